```python
import math
import jax, jax.numpy as jnp
from jax import lax
import numpy as np

D_MODEL = 1024
BATCH = 4
SEQ = 8192
DEPTH = 1

N_MEM = 256
EPS = 1e-6
NEG = -1e30

A_HEADS = 16
A_HEAD_DIM = 64
A_WIDTH = A_HEADS * A_HEAD_DIM
MOBA_BLOCK = 256
MOBA_TOPK = 3
MOBA_Q_CHUNK = 32
ROPE_THETA = 10000.0

B_HEADS = 8
B_KEY_DIM = 128
B_VAL_DIM = D_MODEL // B_HEADS
B_KEY_WIDTH = B_HEADS * B_KEY_DIM
B_VAL_WIDTH = B_HEADS * B_VAL_DIM
HGRN_CHUNK = 64

X_HEADS = 4
X_HEAD_DIM = D_MODEL // X_HEADS

FFN_HIDDEN = ((8 * D_MODEL + 3 * 256 - 1) // (3 * 256)) * 256

IN_WIDTHS = (A_WIDTH, A_WIDTH, A_WIDTH, B_KEY_WIDTH, B_KEY_WIDTH, B_VAL_WIDTH, B_VAL_WIDTH, D_MODEL, D_MODEL)
IN_WIDTH = A_WIDTH * 3 + B_KEY_WIDTH * 2 + B_VAL_WIDTH * 2 + D_MODEL * 2

kernel_name = "hybrid_moba_hgrn2_griffin_merge"


def rmsnorm(x, g):
    xf = x.astype(jnp.float32)
    y = xf * lax.rsqrt(jnp.mean(xf * xf, axis=-1, keepdims=True) + EPS)
    return (y * g.astype(jnp.float32)).astype(x.dtype)


def rope(t, pos):
    hd = t.shape[-1]
    inv_freq = jnp.power(ROPE_THETA, -jnp.arange(0, hd, 2, dtype=jnp.float32) / hd)
    ang = pos.astype(jnp.float32)[:, None, :, None] * inv_freq
    cos, sin = jnp.cos(ang), jnp.sin(ang)
    tf = t.astype(jnp.float32)
    t1, t2 = tf[..., : hd // 2], tf[..., hd // 2:]
    return jnp.concatenate([t1 * cos - t2 * sin, t2 * cos + t1 * sin], axis=-1).astype(t.dtype)


def split_cols(z):
    outs, off = [], 0
    for w in IN_WIDTHS:
        outs.append(z[..., off:off + w])
        off += w
    return outs


def moba_attention(q, k, v, pos):
    B, S, _ = q.shape
    H, hd, BLK, QC = A_HEADS, A_HEAD_DIM, MOBA_BLOCK, MOBA_Q_CHUNK
    scale = hd ** -0.5

    def heads(t):
        return t.reshape(B, S, H, hd).transpose(0, 2, 1, 3)

    q = rope(heads(q), pos)
    k = rope(heads(k), pos)
    v = heads(v)
    nb = -(-S // BLK)
    s_pad = nb * BLK
    padw = [(0, 0), (0, 0), (0, s_pad - S), (0, 0)]
    q, k, v = jnp.pad(q, padw), jnp.pad(k, padw), jnp.pad(v, padw)
    kb = k.reshape(B, H, nb, BLK, hd)
    vb = v.reshape(B, H, nb, BLK, hd)

    k_mean = jnp.mean(kb.astype(jnp.float32), axis=3)
    gate = jnp.einsum('bhsd,bhnd->bhsn', q.astype(jnp.float32), k_mean)
    q_blk = jnp.arange(s_pad) // BLK
    past = jnp.arange(nb)[None, :] < q_blk[:, None]
    gate = jnp.where(past, gate, NEG)
    n_sel = min(MOBA_TOPK, nb)
    _, idx = lax.top_k(gate, n_sel)
    valid = idx < q_blk[:, None]

    n_chunks = s_pad // QC

    def to_chunks(t):
        return jnp.moveaxis(t.reshape(B, H, n_chunks, QC, t.shape[-1]), 2, 0)

    starts = jnp.arange(n_chunks, dtype=jnp.int32) * QC
    bi = jnp.arange(B)[:, None, None, None]
    hi = jnp.arange(H)[None, :, None, None]

    def attend(args):
        qc, ic, vmask, start = args
        blk = start // BLK
        k_own = lax.dynamic_index_in_dim(kb, blk, axis=2, keepdims=False)
        v_own = lax.dynamic_index_in_dim(vb, blk, axis=2, keepdims=False)
        q_pos = start + jnp.arange(QC)
        k_pos = blk * BLK + jnp.arange(BLK)
        s_own = jnp.einsum('bhqd,bhkd->bhqk', qc, k_own).astype(jnp.float32) * scale
        s_own = jnp.where(k_pos[None, :] <= q_pos[:, None], s_own, NEG)
        k_sel = kb[bi, hi, ic]
        v_sel = vb[bi, hi, ic]
        s_sel = jnp.einsum('bhqd,bhqjkd->bhqjk', qc, k_sel).astype(jnp.float32) * scale
        s_sel = jnp.where(vmask[..., None], s_sel, NEG).reshape(B, H, QC, n_sel * BLK)
        p = jax.nn.softmax(jnp.concatenate([s_sel, s_own], axis=-1), axis=-1).astype(v.dtype)
        p_sel = p[..., : n_sel * BLK].reshape(B, H, QC, n_sel, BLK)
        p_own = p[..., n_sel * BLK:]
        return (jnp.einsum('bhqjk,bhqjkd->bhqd', p_sel, v_sel)
                + jnp.einsum('bhqk,bhkd->bhqd', p_own, v_own))

    o = lax.map(attend, (to_chunks(q), to_chunks(idx), to_chunks(valid), starts))
    o = jnp.moveaxis(o, 0, 2).reshape(B, H, s_pad, hd)[:, :, :S]
    return o.transpose(0, 2, 1, 3).reshape(B, S, H * hd)


def hgrn2(q, f, i, og, lb, norm_g):
    B, S, _ = q.shape
    H, dk, dv, C = B_HEADS, B_KEY_DIM, B_VAL_DIM, HGRN_CHUNK
    nc = S // C

    def heads(t, d):
        return t.reshape(B, S, H, d).transpose(0, 2, 1, 3).astype(jnp.float32)

    qh = jax.nn.silu(heads(q, dk)) * (dk ** -0.5)
    lbh = lb.astype(jnp.float32).reshape(1, H, 1, dk)
    fg = lbh + (1.0 - lbh) * jax.nn.sigmoid(heads(f, dk))
    logf = jnp.log(fg)
    kh = 1.0 - fg
    vh = heads(i, dv)

    def to_chunks(t):
        return jnp.moveaxis(t.reshape(B, H, nc, C, t.shape[-1]), 2, 0)

    causal = jnp.tril(jnp.ones((C, C), dtype=bool))[:, :, None]

    def step(state, xs):
        qc, kc, vc, gc = xs
        b = jnp.cumsum(gc, axis=2)
        o_inter = jnp.einsum('bhtd,bhdv->bhtv', qc * jnp.exp(b), state)
        diff = b[:, :, :, None, :] - b[:, :, None, :, :]
        decay = jnp.where(causal, jnp.exp(jnp.where(causal, diff, 0.0)), 0.0)
        a = jnp.einsum('bhtd,bhsd,bhtsd->bhts', qc, kc, decay)
        o = o_inter + jnp.einsum('bhts,bhsv->bhtv', a, vc)
        b_last = b[:, :, -1:, :]
        new_state = (jnp.exp(b_last[:, :, 0, :])[..., None] * state
                     + jnp.einsum('bhsd,bhsv->bhdv', kc * jnp.exp(b_last - b), vc))
        return new_state, o

    s0 = jnp.zeros((B, H, dk, dv), jnp.float32)
    _, o = lax.scan(step, s0, (to_chunks(qh), to_chunks(kh), to_chunks(vh), to_chunks(logf)))
    o = jnp.moveaxis(o, 0, 2).reshape(B, H, S, dv).transpose(0, 2, 1, 3)
    o = o * lax.rsqrt(jnp.mean(o * o, axis=-1, keepdims=True) + EPS)
    o = o.reshape(B, S, H * dv) * norm_g.astype(jnp.float32)
    o = o * jax.nn.silu(og.astype(jnp.float32))
    return o.astype(q.dtype)


def memory_cross_attention(h, mem, wq, wkv, wo):
    B, S, _ = h.shape
    M = mem.shape[1]
    q = (h @ wq).reshape(B, S, X_HEADS, X_HEAD_DIM)
    kv = mem @ wkv
    k = kv[..., :D_MODEL].reshape(B, M, X_HEADS, X_HEAD_DIM)
    v = kv[..., D_MODEL:].reshape(B, M, X_HEADS, X_HEAD_DIM)
    s = jnp.einsum('bshd,bmhd->bhsm', q, k).astype(jnp.float32) * (X_HEAD_DIM ** -0.5)
    p = jax.nn.softmax(s, axis=-1).astype(v.dtype)
    o = jnp.einsum('bhsm,bmhd->bshd', p, v).reshape(B, S, D_MODEL)
    return o @ wo


def setup_inputs(seed: int = 0) -> dict:
    key = jax.random.key(seed)
    ks = jax.random.split(key, 24)
    f32 = jnp.float32

    def w(k, shape, fan_in):
        return jax.random.normal(k, shape, f32) * (fan_in ** -0.5)

    def gain(k, shape):
        return 1.0 + 0.02 * jax.random.normal(k, shape, f32)

    x = jax.random.normal(ks[0], (BATCH, SEQ, D_MODEL), f32)
    mem = jax.random.normal(ks[1], (BATCH, N_MEM, D_MODEL), f32)
    offs = jax.random.randint(ks[2], (BATCH, 1), 0, 1024, dtype=jnp.int32)
    positions = offs + jnp.arange(SEQ, dtype=jnp.int32)[None, :]
    return {
        "x": x,
        "mem": mem,
        "positions": positions,
        "norm_mix_g": gain(ks[3], (DEPTH, D_MODEL)),
        "w_in": w(ks[4], (DEPTH, D_MODEL, IN_WIDTH), D_MODEL),
        "hgrn_lb_logits": 0.5 * jax.random.normal(ks[5], (DEPTH + 1, B_KEY_WIDTH), f32),
        "hgrn_norm_g": gain(ks[6], (DEPTH, B_VAL_WIDTH)),
        "w_br_attn": w(ks[7], (DEPTH, A_WIDTH, D_MODEL), A_WIDTH),
        "w_br_hgrn": w(ks[8], (DEPTH, B_VAL_WIDTH, D_MODEL), B_VAL_WIDTH),
        "w_out": w(ks[9], (DEPTH, D_MODEL, D_MODEL), D_MODEL),
        "norm_x_g": gain(ks[10], (DEPTH, D_MODEL)),
        "norm_mem_g": gain(ks[11], (DEPTH, D_MODEL)),
        "wq_x": w(ks[12], (DEPTH, D_MODEL, D_MODEL), D_MODEL),
        "wkv_x": w(ks[13], (DEPTH, D_MODEL, 2 * D_MODEL), D_MODEL),
        "wo_x": w(ks[14], (DEPTH, D_MODEL, D_MODEL), D_MODEL),
        "norm_ffn_g": gain(ks[15], (DEPTH, D_MODEL)),
        "w_ffn_in": w(ks[16], (DEPTH, D_MODEL, 2 * FFN_HIDDEN), D_MODEL),
        "w_ffn_out": w(ks[17], (DEPTH, FFN_HIDDEN, D_MODEL), FFN_HIDDEN),
        "final_norm_g": gain(ks[18], (D_MODEL,)),
    }


def reference(x, mem, positions, norm_mix_g, w_in, hgrn_lb_logits, hgrn_norm_g, w_br_attn,
              w_br_hgrn, w_out, norm_x_g, norm_mem_g, wq_x, wkv_x, wo_x, norm_ffn_g,
              w_ffn_in, w_ffn_out, final_norm_g):
    lb_all = jnp.cumsum(jax.nn.softmax(hgrn_lb_logits.astype(jnp.float32), axis=0), axis=0)
    h = x
    for l in range(DEPTH):
        xn = rmsnorm(h, norm_mix_g[l])
        q_a, k_a, v_a, q_b, f_b, i_b, og_b, g_a, g_b = split_cols(xn @ w_in[l])
        y_a = moba_attention(q_a, k_a, v_a, positions) @ w_br_attn[l]
        y_b = hgrn2(q_b, f_b, i_b, og_b, lb_all[l], hgrn_norm_g[l]) @ w_br_hgrn[l]
        merged = jax.nn.sigmoid(g_a) * y_a + jax.nn.sigmoid(g_b) * y_b
        h = h + merged @ w_out[l]
        h = h + memory_cross_attention(rmsnorm(h, norm_x_g[l]), rmsnorm(mem, norm_mem_g[l]),
                                       wq_x[l], wkv_x[l], wo_x[l])
        hn = rmsnorm(h, norm_ffn_g[l])
        gu = hn @ w_ffn_in[l]
        h = h + (jax.nn.silu(gu[..., :FFN_HIDDEN]) * gu[..., FFN_HIDDEN:]) @ w_ffn_out[l]
    return rmsnorm(h, final_norm_g)
```

```python
import functools

import jax
import jax.numpy as jnp
from jax import lax
from jax.experimental import pallas as pl
from jax.experimental.pallas import tpu as pltpu

F32 = jnp.float32
BF16 = jnp.bfloat16

EPS = 1e-6
NEG = -1e30
ROPE_THETA = 10000.0

A_HEADS = 16
A_HEAD_DIM = 64
MOBA_BLOCK = 256
MOBA_TOPK = 3
B_HEADS = 8
B_KEY_DIM = 128
X_HEADS = 4
HGRN_CHUNK = 64
HGRN_SUB = 16

LANES = 128
VMEM_LIMIT = 56 * 1024 * 1024


def _params(sem):
    return pltpu.CompilerParams(dimension_semantics=sem, vmem_limit_bytes=VMEM_LIMIT)


def _rms(x, g):
    return x * lax.rsqrt(jnp.mean(x * x, axis=-1, keepdims=True) + EPS) * g


def _dot(a, b):
    return jnp.dot(a, b, preferred_element_type=F32)


def _dot_nt(a, b):
    return lax.dot_general(a, b, (((1,), (1,)), ((), ())), preferred_element_type=F32)


def _sigmoid(x):
    return 1.0 / (1.0 + jnp.exp(-x))


def _memkv_kernel(mem_ref, g_ref, w_ref, o_ref):
    y = _rms(mem_ref[...], g_ref[...])
    o_ref[...] = _dot(y.astype(BF16), w_ref[...]).astype(BF16)


def _memkv(mem2d, g, w, n_mem):
    rows, d = mem2d.shape
    return pl.pallas_call(
        _memkv_kernel,
        grid=(rows // n_mem,),
        in_specs=[
            pl.BlockSpec((n_mem, d), lambda b: (b, 0)),
            pl.BlockSpec((1, d), lambda b: (0, 0)),
            pl.BlockSpec(w.shape, lambda b: (0, 0)),
        ],
        out_specs=pl.BlockSpec((n_mem, w.shape[1]), lambda b: (b, 0)),
        out_shape=jax.ShapeDtypeStruct((rows, w.shape[1]), BF16),
        compiler_params=_params(("parallel",)),
        name="memkv",
    )(mem2d, g, w)


def _inproj_kernel(x_ref, g_ref, w_ref, cos_ref, sin_ref, o_ref, xn_ref, *, tn):
    j = pl.program_id(1)

    @pl.when(j == 0)
    def _():
        xn_ref[...] = _rms(x_ref[...], g_ref[...]).astype(BF16)

    acc = _dot(xn_ref[...], w_ref[...])

    @pl.when(j < 2)
    def _():
        scale = jnp.where(j == 0, A_HEAD_DIM ** -0.5, 1.0).astype(F32)
        cos = cos_ref[...]
        sin = sin_ref[...]
        lane = lax.broadcasted_iota(jnp.int32, cos.shape, 1)
        low = (lane % A_HEAD_DIM) < (A_HEAD_DIM // 2)
        for c in range(tn // LANES):
            t = acc[:, c * LANES:(c + 1) * LANES]
            rot = jnp.where(low, pltpu.roll(t, LANES - A_HEAD_DIM // 2, 1), pltpu.roll(t, A_HEAD_DIM // 2, 1))
            o_ref[:, c * LANES:(c + 1) * LANES] = ((t * cos + rot * sin) * scale).astype(BF16)

    @pl.when(j >= 2)
    def _():
        o_ref[...] = acc.astype(BF16)


def _inproj(x2d, g, w, cos_t, sin_t, tm, tn):
    t, d = x2d.shape
    n = w.shape[1]
    return pl.pallas_call(
        functools.partial(_inproj_kernel, tn=tn),
        grid=(t // tm, n // tn),
        in_specs=[
            pl.BlockSpec((tm, d), lambda i, j: (i, 0)),
            pl.BlockSpec((1, d), lambda i, j: (0, 0)),
            pl.BlockSpec((d, tn), lambda i, j: (0, j)),
            pl.BlockSpec((tm, LANES), lambda i, j: (i, 0)),
            pl.BlockSpec((tm, LANES), lambda i, j: (i, 0)),
        ],
        out_specs=pl.BlockSpec((tm, tn), lambda i, j: (i, j)),
        out_shape=jax.ShapeDtypeStruct((t, n), BF16),
        scratch_shapes=[pltpu.VMEM((tm, d), BF16)],
        compiler_params=_params(("parallel", "arbitrary")),
        name="inproj",
    )(x2d, g, w, cos_t, sin_t)


def _moba_kernel(q_ref, k_ref, v_ref, o_ref, vt_ref, km_ref, sel_ref, m_ref, l_ref, acc_ref, *, nb, blk):
    i = pl.program_id(2)
    hd = A_HEAD_DIM

    @pl.when(i == 0)
    def _prep():
        for n in range(nb):
            vt_ref[n] = v_ref[n * blk:(n + 1) * blk, :].astype(F32).T.astype(BF16)
            km_ref[n:n + 1, :] = jnp.mean(k_ref[n * blk:(n + 1) * blk, :].astype(F32), axis=0, keepdims=True)

    q_t = q_ref[...].astype(F32).T
    row = lax.broadcasted_iota(jnp.int32, q_t.shape, 0)
    qz = [jnp.where(row < hd, q_t, 0.0).astype(BF16), jnp.where(row >= hd, q_t, 0.0).astype(BF16)]

    km = km_ref[...]
    km_hi = km.astype(BF16)
    km_lo = (km - km_hi.astype(F32)).astype(BF16)
    blk_id = lax.broadcasted_iota(jnp.int32, (nb, blk), 0)
    for h in range(2):
        g = _dot(km_hi, qz[h]) + _dot(km_lo, qz[h])
        g = jnp.where(blk_id < i, g, -jnp.inf)
        sel = jnp.zeros((nb, blk), F32)
        for _ in range(MOBA_TOPK):
            mx = jnp.max(g, axis=0, keepdims=True)
            first = jnp.min(jnp.where(g == mx, blk_id, nb), axis=0, keepdims=True)
            pick = jnp.logical_and(blk_id == first, mx > -jnp.inf)
            sel = jnp.where(pick, 1.0, sel)
            g = jnp.where(blk_id == first, -jnp.inf, g)
        sel_ref[h] = sel

    m_ref[...] = jnp.full(m_ref.shape, NEG, F32)
    l_ref[...] = jnp.zeros(l_ref.shape, F32)
    acc_ref[...] = jnp.zeros(acc_ref.shape, F32)

    kidx = lax.broadcasted_iota(jnp.int32, (blk, blk), 0)
    qidx = lax.broadcasted_iota(jnp.int32, (blk, blk), 1)

    def tile(j, diag):
        r = pl.multiple_of(j * blk, blk)
        kj = k_ref[pl.ds(r, blk), :]
        vj = vt_ref[j]
        for h in range(2):
            s = _dot(kj, qz[h])
            m_old = m_ref[h:h + 1, :]
            if diag:
                s = jnp.where(kidx <= qidx, s, NEG)
                m_new = jnp.maximum(m_old, jnp.max(s, axis=0, keepdims=True))
                m_use = m_new
            else:
                on = sel_ref[h, pl.ds(j, 1), :] > 0.0
                m_new = jnp.where(on, jnp.maximum(m_old, jnp.max(s, axis=0, keepdims=True)), m_old)
                m_use = jnp.where(on, m_new, -NEG)
            p = jnp.exp(s - m_use)
            alpha = jnp.exp(m_old - m_new)
            l_ref[h:h + 1, :] = alpha * l_ref[h:h + 1, :] + jnp.sum(p, axis=0, keepdims=True)
            acc_ref[h] = alpha * acc_ref[h] + _dot(vj[h * hd:(h + 1) * hd, :], p.astype(BF16))
            m_ref[h:h + 1, :] = m_new

    def body(j, c):
        tile(j, False)
        return c

    lax.fori_loop(0, i, body, 0)
    tile(i, True)

    out_t = jnp.concatenate([acc_ref[h] / l_ref[h:h + 1, :] for h in range(2)], axis=0)
    o_ref[...] = out_t.T.astype(BF16)


def _moba(z, batch, seq):
    blk = MOBA_BLOCK
    nb = seq // blk
    cb = (A_HEADS * A_HEAD_DIM) // LANES
    return pl.pallas_call(
        functools.partial(_moba_kernel, nb=nb, blk=blk),
        grid=(batch, cb, nb),
        in_specs=[
            pl.BlockSpec((blk, LANES), lambda b, p, i: (b * nb + i, p)),
            pl.BlockSpec((seq, LANES), lambda b, p, i: (b, cb + p)),
            pl.BlockSpec((seq, LANES), lambda b, p, i: (b, 2 * cb + p)),
        ],
        out_specs=pl.BlockSpec((blk, LANES), lambda b, p, i: (b * nb + i, p)),
        out_shape=jax.ShapeDtypeStruct((batch * seq, cb * LANES), BF16),
        scratch_shapes=[
            pltpu.VMEM((nb, LANES, blk), BF16),
            pltpu.VMEM((nb, LANES), F32),
            pltpu.VMEM((2, nb, blk), F32),
            pltpu.VMEM((2, blk), F32),
            pltpu.VMEM((2, blk), F32),
            pltpu.VMEM((2, A_HEAD_DIM, blk), F32),
        ],
        compiler_params=_params(("parallel", "parallel", "arbitrary")),
        name="moba",
    )(z, z, z)


def _hgrn_kernel(q_ref, f_ref, i_ref, og_ref, lbl_ref, ng_ref, o_ref, st_ref, *, lt):
    c_len, sub = HGRN_CHUNK, HGRN_SUB
    n_sub = c_len // sub
    dk = B_KEY_DIM

    @pl.when(pl.program_id(2) == 0)
    def _():
        st_ref[...] = jnp.zeros(st_ref.shape, F32)

    lg = lbl_ref[...]
    e = jnp.exp(lg - jnp.max(lg, axis=0, keepdims=True))
    lb = e[0:1, :] / jnp.sum(e, axis=0, keepdims=True)
    ng = ng_ref[...]

    tri = (lax.broadcasted_iota(jnp.int32, (c_len, c_len), 0)
           >= lax.broadcasted_iota(jnp.int32, (c_len, c_len), 1)).astype(BF16)
    rowi = lax.broadcasted_iota(jnp.int32, (n_sub, sub, dk), 1)

    def chunk(c, carry):
        r = pl.multiple_of(c * c_len, c_len)
        qb = q_ref[pl.ds(r, c_len), :].astype(F32)
        fb = f_ref[pl.ds(r, c_len), :].astype(F32)
        v = i_ref[pl.ds(r, c_len), :].astype(F32)
        og = og_ref[pl.ds(r, c_len), :].astype(F32)

        q = qb * _sigmoid(qb) * (dk ** -0.5)
        fg = lb + (1.0 - lb) * _sigmoid(fb)
        logf = jnp.log(fg)
        k = 1.0 - fg

        hi = logf.astype(BF16)
        r1 = logf - hi.astype(F32)
        mid = r1.astype(BF16)
        lo = (r1 - mid.astype(F32)).astype(BF16)
        b = _dot(tri, hi) + _dot(tri, mid) + _dot(tri, lo)

        st = st_ref[...]
        o = _dot_nt((q * jnp.exp(b)).astype(BF16), st.astype(BF16))

        v16 = v.astype(BF16)
        parts = [jnp.zeros((sub, v.shape[1]), F32)]
        for s_i in range(1, n_sub):
            lo_r = s_i * sub
            ref_b = b[lo_r - 1:lo_r, :]
            qt = q[lo_r:lo_r + sub, :] * jnp.exp(b[lo_r:lo_r + sub, :] - ref_b)
            kt = k[:lo_r, :] * jnp.exp(ref_b - b[:lo_r, :])
            a = _dot_nt(qt.astype(BF16), kt.astype(BF16))
            parts.append(_dot(a.astype(BF16), v16[:lo_r, :]))
        o = o + jnp.concatenate(parts, axis=0)

        b3 = b.reshape(n_sub, sub, dk)
        q3 = q.reshape(n_sub, sub, dk)
        k3 = k.reshape(n_sub, sub, dk)
        v3 = v.reshape(n_sub, sub, v.shape[1])
        od = jnp.zeros(v3.shape, F32)
        for s_o in range(sub):
            dec = jnp.exp(jnp.minimum(b3 - b3[:, s_o:s_o + 1, :], 0.0))
            gq = jnp.where(rowi >= s_o, q3 * dec, 0.0) * k3[:, s_o:s_o + 1, :]
            od = od + jnp.sum(gq, axis=-1, keepdims=True) * v3[:, s_o:s_o + 1, :]
        o = o + od.reshape(c_len, v.shape[1])

        b_last = b[c_len - 1:c_len, :]
        kl = k * jnp.exp(b_last - b)
        st_ref[...] = st * jnp.exp(b_last) + _dot(v.T.astype(BF16), kl.astype(BF16))

        o = o * lax.rsqrt(jnp.mean(o * o, axis=-1, keepdims=True) + EPS) * ng
        o = o * (og * _sigmoid(og))
        o_ref[pl.ds(r, c_len), :] = o.astype(BF16)
        return carry

    lax.fori_loop(0, lt // c_len, chunk, 0)


def _hgrn(z, lb_logits, norm_g, batch, seq, lt):
    nt = seq // lt
    cq = (3 * A_HEADS * A_HEAD_DIM) // LANES
    h = B_HEADS
    seg = lambda off: pl.BlockSpec((lt, LANES), lambda b, hh, t: (b * nt + t, cq + off * h + hh))
    return pl.pallas_call(
        functools.partial(_hgrn_kernel, lt=lt),
        grid=(batch, h, nt),
        in_specs=[
            seg(0), seg(1), seg(2), seg(3),
            pl.BlockSpec((lb_logits.shape[0], LANES), lambda b, hh, t: (0, hh)),
            pl.BlockSpec((1, LANES), lambda b, hh, t: (0, hh)),
        ],
        out_specs=pl.BlockSpec((lt, LANES), lambda b, hh, t: (b * nt + t, hh)),
        out_shape=jax.ShapeDtypeStruct((batch * seq, h * LANES), BF16),
        scratch_shapes=[pltpu.VMEM((LANES, B_KEY_DIM), F32)],
        compiler_params=_params(("parallel", "parallel", "arbitrary")),
        name="hgrn",
    )(z, z, z, z, lb_logits, norm_g)


def _merge_kernel(x_ref, a_ref, hg_ref, ga_ref, gb_ref, wa_ref, wb_ref, wo_ref, gx_ref, wq_ref,
                  km_ref, vm_ref, wox_ref, o_ref):
    ya = _dot(a_ref[...], wa_ref[...])
    yb = _dot(hg_ref[...], wb_ref[...])
    mg = _sigmoid(ga_ref[...].astype(F32)) * ya + _sigmoid(gb_ref[...].astype(F32)) * yb
    h = x_ref[...] + _dot(mg.astype(BF16), wo_ref[...])

    hn = _rms(h, gx_ref[...]).astype(BF16)
    q = _dot(hn, wq_ref[...])
    d = q.shape[1]
    xd = d // X_HEADS
    outs = []
    for hh in range(X_HEADS):
        sl = slice(hh * xd, (hh + 1) * xd)
        s = _dot_nt(q[:, sl].astype(BF16), km_ref[:, sl]) * (xd ** -0.5)
        p = jnp.exp(s - jnp.max(s, axis=-1, keepdims=True))
        oh = _dot(p.astype(BF16), vm_ref[:, sl])
        outs.append(oh / jnp.sum(p, axis=-1, keepdims=True))
    o = jnp.concatenate(outs, axis=-1)
    o_ref[...] = h + _dot(o.astype(BF16), wox_ref[...])


def _merge(x2d, attn, hg, z, wa, wb, wo, gx, wq, kvm, wox, batch, seq, n_mem, tm):
    d = x2d.shape[1]
    nt = seq // tm
    gcol = (z.shape[1] - 2 * d) // d
    row = lambda b, t: (b * nt + t, 0)
    full = lambda b, t: (0, 0)
    wspec = pl.BlockSpec((d, d), full)
    return pl.pallas_call(
        _merge_kernel,
        grid=(batch, nt),
        in_specs=[
            pl.BlockSpec((tm, d), row),
            pl.BlockSpec((tm, d), row),
            pl.BlockSpec((tm, d), row),
            pl.BlockSpec((tm, d), lambda b, t: (b * nt + t, gcol)),
            pl.BlockSpec((tm, d), lambda b, t: (b * nt + t, gcol + 1)),
            wspec, wspec, wspec,
            pl.BlockSpec((1, d), full),
            wspec,
            pl.BlockSpec((n_mem, d), lambda b, t: (b, 0)),
            pl.BlockSpec((n_mem, d), lambda b, t: (b, 1)),
            wspec,
        ],
        out_specs=pl.BlockSpec((tm, d), row),
        out_shape=jax.ShapeDtypeStruct(x2d.shape, F32),
        compiler_params=_params(("parallel", "parallel")),
        name="merge",
    )(x2d, attn, hg, z, z, wa, wb, wo, gx, wq, kvm, kvm, wox)


def _ffn_kernel(h_ref, g_ref, wi_ref, wo_ref, gf_ref, o_ref, *, hidden, bounds):
    h = h_ref[...]
    hn = _rms(h, g_ref[...]).astype(BF16)
    acc = h
    for lo, hi in bounds:
        gate = _dot(hn, wi_ref[:, lo:hi])
        up = _dot(hn, wi_ref[:, hidden + lo:hidden + hi])
        act = (gate * _sigmoid(gate) * up).astype(BF16)
        acc = acc + _dot(act, wo_ref[lo:hi, :])
    o_ref[...] = _rms(acc, gf_ref[...])


def _ffn(h2d, g, wi, wo, gf, tm):
    t, d = h2d.shape
    hidden = wo.shape[0]
    step = 1024
    bounds = tuple((lo, min(lo + step, hidden)) for lo in range(0, hidden, step))
    full = lambda i: (0, 0)
    return pl.pallas_call(
        functools.partial(_ffn_kernel, hidden=hidden, bounds=bounds),
        grid=(t // tm,),
        in_specs=[
            pl.BlockSpec((tm, d), lambda i: (i, 0)),
            pl.BlockSpec((1, d), full),
            pl.BlockSpec(wi.shape, full, pipeline_mode=pl.Buffered(1)),
            pl.BlockSpec(wo.shape, full, pipeline_mode=pl.Buffered(1)),
            pl.BlockSpec((1, d), full),
        ],
        out_specs=pl.BlockSpec((tm, d), lambda i: (i, 0)),
        out_shape=jax.ShapeDtypeStruct(h2d.shape, F32),
        compiler_params=_params(("parallel",)),
        name="ffn",
    )(h2d, g, wi, wo, gf)


def _rope_tables(positions):
    half = A_HEAD_DIM // 2
    inv_freq = jnp.power(ROPE_THETA, -jnp.arange(0, A_HEAD_DIM, 2, dtype=F32) / A_HEAD_DIM)
    ang = positions.astype(F32).reshape(-1, 1) * inv_freq
    cos, sin = jnp.cos(ang), jnp.sin(ang)
    reps = LANES // A_HEAD_DIM
    cos_t = jnp.tile(jnp.concatenate([cos, cos], axis=-1), (1, reps))
    sin_t = jnp.tile(jnp.concatenate([-sin, sin], axis=-1), (1, reps))
    assert cos_t.shape[1] == LANES and half * 2 == A_HEAD_DIM
    return cos_t, sin_t


def kernel(x, mem, positions, norm_mix_g, w_in, hgrn_lb_logits, hgrn_norm_g, w_br_attn, w_br_hgrn, w_out,
           norm_x_g, norm_mem_g, wq_x, wkv_x, wo_x, norm_ffn_g, w_ffn_in, w_ffn_out, final_norm_g):
    batch, seq, d = x.shape
    n_mem = mem.shape[1]
    depth = w_in.shape[0]
    assert depth == 1 and d == A_HEADS * A_HEAD_DIM == B_HEADS * LANES and B_KEY_DIM == LANES
    assert seq % MOBA_BLOCK == 0 and w_in.shape[2] == 9 * d

    tm = min(512, seq)
    lt = min(1024, seq)
    x2d = x.reshape(batch * seq, d)
    cos_t, sin_t = _rope_tables(positions)
    row = lambda g: g.reshape(1, -1).astype(F32)
    bf = lambda w: w.astype(BF16)

    kvm = _memkv(mem.reshape(batch * n_mem, d), row(norm_mem_g[0]), bf(wkv_x[0]), n_mem)
    z = _inproj(x2d, row(norm_mix_g[0]), bf(w_in[0]), cos_t, sin_t, tm, d)
    attn = _moba(z, batch, seq)
    hg = _hgrn(z, hgrn_lb_logits.astype(F32), row(hgrn_norm_g[0]), batch, seq, lt)
    h = _merge(x2d, attn, hg, z, bf(w_br_attn[0]), bf(w_br_hgrn[0]), bf(w_out[0]), row(norm_x_g[0]),
               bf(wq_x[0]), kvm, bf(wo_x[0]), batch, seq, n_mem, tm)
    out = _ffn(h, row(norm_ffn_g[0]), bf(w_ffn_in[0]), bf(w_ffn_out[0]), row(final_norm_g), tm)
    return out.reshape(batch, seq, d)
```

```python
import functools

import jax
import jax.numpy as jnp
from jax import lax
from jax.experimental import pallas as pl
from jax.experimental.pallas import tpu as pltpu

F32 = jnp.float32
BF16 = jnp.bfloat16

EPS = 1e-6
NEG = -1e30
ROPE_THETA = 10000.0

A_HEADS = 16
A_HEAD_DIM = 64
MOBA_BLOCK = 256
MOBA_TOPK = 3
MOBA_QUERY_BLOCKS = 4
MOBA_LOOKAHEAD = 3
B_HEADS = 8
B_KEY_DIM = 128
X_HEADS = 4
HGRN_CHUNK = 64
HGRN_SUB = 16
HGRN_GROUP = 4

LANES = 128
VMEM_LIMIT = 56 * 1024 * 1024


def _params(sem, flags=None):
    return pltpu.CompilerParams(dimension_semantics=sem, vmem_limit_bytes=VMEM_LIMIT, flags=flags)


def _rms(x, g):
    return x * lax.rsqrt(jnp.mean(x * x, axis=-1, keepdims=True) + EPS) * g


def _dot(a, b):
    return jnp.dot(a, b, preferred_element_type=F32)


def _dot_nt(a, b):
    return lax.dot_general(a, b, (((1,), (1,)), ((), ())), preferred_element_type=F32)


def _sigmoid(x):
    return 1.0 / (1.0 + jnp.exp(-x))


def _memkv_kernel(mem_ref, g_ref, w_ref, o_ref):
    y = _rms(mem_ref[...], g_ref[...])
    o_ref[...] = _dot(y.astype(BF16), w_ref[...]).astype(BF16)


def _memkv(mem2d, g, w, n_mem):
    rows, d = mem2d.shape
    return pl.pallas_call(
        _memkv_kernel,
        grid=(rows // n_mem,),
        in_specs=[
            pl.BlockSpec((n_mem, d), lambda b: (b, 0)),
            pl.BlockSpec((1, d), lambda b: (0, 0)),
            pl.BlockSpec(w.shape, lambda b: (0, 0)),
        ],
        out_specs=pl.BlockSpec((n_mem, w.shape[1]), lambda b: (b, 0)),
        out_shape=jax.ShapeDtypeStruct((rows, w.shape[1]), BF16),
        compiler_params=_params(("parallel",)),
        name="memkv",
    )(mem2d, g, w)


def _inproj_kernel(x_ref, g_ref, w_ref, wt_ref, cos_ref, sin_ref, cost_ref, sint_ref,
                   z_ref, qt_ref, vt_ref, xn_ref, *, tm):
    j = pl.program_id(1)
    hd, half = A_HEAD_DIM, A_HEAD_DIM // 2

    @pl.when(j == 0)
    def _():
        xn_ref[...] = _rms(x_ref[...], g_ref[...]).astype(BF16)
        acc = _dot_nt(wt_ref[...], xn_ref[...])
        cos = cost_ref[...]
        sin = sint_ref[...]
        scale = hd ** -0.5
        for h in range(acc.shape[0] // hd):
            t1 = acc[h * hd:h * hd + half, :]
            t2 = acc[h * hd + half:(h + 1) * hd, :]
            qt_ref[h * hd:h * hd + half, :] = ((t1 * cos - t2 * sin) * scale).astype(BF16)
            qt_ref[h * hd + half:(h + 1) * hd, :] = ((t2 * cos + t1 * sin) * scale).astype(BF16)

    @pl.when(j == 1)
    def _():
        acc = _dot(xn_ref[...], w_ref[...])
        cos = cos_ref[...]
        sin = sin_ref[...]
        lane = lax.broadcasted_iota(jnp.int32, cos.shape, 1)
        low = (lane % hd) < half
        for c in range(acc.shape[1] // LANES):
            t = acc[:, c * LANES:(c + 1) * LANES]
            rot = jnp.where(low, pltpu.roll(t, LANES - half, 1), pltpu.roll(t, half, 1))
            z_ref[:, c * LANES:(c + 1) * LANES] = (t * cos + rot * sin).astype(BF16)

    @pl.when(j == 2)
    def _():
        acc = _dot_nt(wt_ref[...], xn_ref[...]).astype(BF16)
        for n in range(tm // MOBA_BLOCK):
            vt_ref[n] = acc[:, n * MOBA_BLOCK:(n + 1) * MOBA_BLOCK]

    @pl.when(j >= 3)
    def _():
        z_ref[...] = _dot(xn_ref[...], w_ref[...]).astype(BF16)


def _inproj(x2d, g, w, wt, cos_t, sin_t, cos_tt, sin_tt, tm):
    t, d = x2d.shape
    nseg = w.shape[1] // d
    blk = MOBA_BLOCK
    w_col = lambda i, j: (0, jnp.where(j == 0, 1, jnp.where(j == 2, 3, j)))
    z_col = lambda i, j: (i, jnp.where(j <= 2, 0, j - 2))
    return pl.pallas_call(
        functools.partial(_inproj_kernel, tm=tm),
        grid=(t // tm, nseg),
        in_specs=[
            pl.BlockSpec((tm, d), lambda i, j: (i, 0)),
            pl.BlockSpec((1, d), lambda i, j: (0, 0)),
            pl.BlockSpec((d, d), w_col),
            pl.BlockSpec((None, d, d), lambda i, j: (jnp.where(j < 2, 0, 1), 0, 0)),
            pl.BlockSpec((tm, LANES), lambda i, j: (i, 0)),
            pl.BlockSpec((tm, LANES), lambda i, j: (i, 0)),
            pl.BlockSpec((A_HEAD_DIM // 2, tm), lambda i, j: (0, i)),
            pl.BlockSpec((A_HEAD_DIM // 2, tm), lambda i, j: (0, i)),
        ],
        out_specs=[
            pl.BlockSpec((tm, d), z_col),
            pl.BlockSpec((d, tm), lambda i, j: (0, i)),
            pl.BlockSpec((tm // blk, d, blk), lambda i, j: (i, 0, 0)),
        ],
        out_shape=[
            jax.ShapeDtypeStruct((t, (nseg - 2) * d), BF16),
            jax.ShapeDtypeStruct((d, t), BF16),
            jax.ShapeDtypeStruct((t // blk, d, blk), BF16),
        ],
        scratch_shapes=[pltpu.VMEM((tm, d), BF16)],
        compiler_params=_params(("parallel", "arbitrary")),
        name="inproj",
    )(x2d, g, w, wt, cos_t, sin_t, cos_tt, sin_tt)


def _moba_kernel(q_ref, k_ref, v_ref, o_ref, km_ref, sel_ref, m_ref, l_ref, acc_ref, *, nb, blk, tq):
    i = pl.program_id(2)
    hd = A_HEAD_DIM
    halves = tq // blk

    @pl.when(i == 0)
    def _prep():
        for n in range(nb):
            km_ref[n:n + 1, :] = jnp.mean(k_ref[n * blk:(n + 1) * blk, :].astype(F32), axis=0, keepdims=True)

    q_t = q_ref[...]
    row = lax.broadcasted_iota(jnp.int32, q_t.shape, 0)
    zero = jnp.zeros_like(q_t)
    qz = [jnp.where(row < hd, q_t, zero), jnp.where(row >= hd, q_t, zero)]

    km = km_ref[...]
    km_hi = km.astype(BF16)
    km_lo = (km - km_hi.astype(F32)).astype(BF16)
    blk_id = lax.broadcasted_iota(jnp.int32, (nb, tq), 0)
    q_blk = i * halves + lax.broadcasted_iota(jnp.int32, (nb, tq), 1) // blk
    for h in range(2):
        g = _dot(km_hi, qz[h]) + _dot(km_lo, qz[h])
        g = jnp.where(blk_id < q_blk, g, -jnp.inf)
        sel = jnp.zeros((nb, tq), F32)
        for _ in range(MOBA_TOPK):
            mx = jnp.max(g, axis=0, keepdims=True)
            first = jnp.min(jnp.where(g == mx, blk_id, nb), axis=0, keepdims=True)
            pick = jnp.logical_and(blk_id == first, mx > -jnp.inf)
            sel = jnp.where(pick, 1.0, sel)
            g = jnp.where(blk_id == first, -jnp.inf, g)
        sel_ref[h] = sel

    m_ref[...] = jnp.full(m_ref.shape, NEG, F32)
    l_ref[...] = jnp.zeros(l_ref.shape, F32)
    acc_ref[...] = jnp.zeros(acc_ref.shape, F32)

    causal = (lax.broadcasted_iota(jnp.int32, (blk, blk), 0)
              <= lax.broadcasted_iota(jnp.int32, (blk, blk), 1))

    def score(h, c, items):
        cs = slice(c * blk, (c + 1) * blk)
        qc = qz[h][:, cs]
        scored = []
        for j, own in items:
            kj = k_ref[pl.ds(pl.multiple_of(j * blk, blk), blk), :]
            s = _dot(kj, qc)
            on = None if own else sel_ref[h, pl.ds(j, 1), cs] > 0.0
            scored.append((j, jnp.where(causal, s, NEG) if own else s, on))
        return scored

    def finish(h, c, scored):
        cs = slice(c * blk, (c + 1) * blk)
        m_old = m_ref[h:h + 1, cs]
        m_new = m_old
        for j, s, on in scored:
            m_blk = jnp.maximum(m_new, jnp.max(s, axis=0, keepdims=True))
            m_new = m_blk if on is None else jnp.where(on, m_blk, m_new)
        alpha = jnp.exp(m_old - m_new)
        l_new = alpha * l_ref[h:h + 1, cs]
        acc = alpha * acc_ref[h, :, cs]
        for j, s, on in scored:
            m_use = m_new if on is None else jnp.where(on, m_new, -NEG)
            p = jnp.exp(s - m_use)
            l_new = l_new + jnp.sum(p, axis=0, keepdims=True)
            acc = acc + _dot(v_ref[j, h * hd:(h + 1) * hd, :], p.astype(BF16))
        m_ref[h:h + 1, cs] = m_new
        l_ref[h:h + 1, cs] = l_new
        acc_ref[h, :, cs] = acc

    groups = [(h, c) for h in range(2) for c in range(halves)]

    def sweep(items_of):
        pending = []
        for h, c in groups:
            pending.append((h, c, score(h, c, items_of(c))))
            if len(pending) > MOBA_LOOKAHEAD:
                finish(*pending.pop(0))
        for item in pending:
            finish(*item)

    def body(jj, carry):
        sweep(lambda c: [(jj * halves + t, False) for t in range(halves)])
        return carry

    lax.fori_loop(0, i, body, 0)
    sweep(lambda c: [(i * halves + t, t == c) for t in range(c + 1)])

    out_t = jnp.concatenate([acc_ref[h] / l_ref[h:h + 1, :] for h in range(2)], axis=0)
    o_ref[...] = out_t.T.astype(BF16)


def _moba(z, qt, vt, batch, seq, tq):
    blk = MOBA_BLOCK
    nb = seq // blk
    nq = seq // tq
    d = qt.shape[0]
    pairs = d // LANES
    return pl.pallas_call(
        functools.partial(_moba_kernel, nb=nb, blk=blk, tq=tq),
        grid=(batch, pairs, nq),
        in_specs=[
            pl.BlockSpec((LANES, tq), lambda b, p, i: (p, b * nq + i)),
            pl.BlockSpec((seq, LANES), lambda b, p, i: (b, p)),
            pl.BlockSpec((nb, LANES, blk), lambda b, p, i: (b, p, 0)),
        ],
        out_specs=pl.BlockSpec((tq, LANES), lambda b, p, i: (b * nq + i, p)),
        out_shape=jax.ShapeDtypeStruct((batch * seq, d), BF16),
        scratch_shapes=[
            pltpu.VMEM((nb, LANES), F32),
            pltpu.VMEM((2, nb, tq), F32),
            pltpu.VMEM((2, tq), F32),
            pltpu.VMEM((2, tq), F32),
            pltpu.VMEM((2, A_HEAD_DIM, tq), F32),
        ],
        compiler_params=_params(("parallel", "parallel", "arbitrary")),
        name="moba",
    )(qt, z, vt)


def _hgrn_kernel(q_ref, f_ref, i_ref, og_ref, lbl_ref, ng_ref, o_ref, st_ref, *, lt, group):
    c_len, sub = HGRN_CHUNK, HGRN_SUB
    n_sub = c_len // sub
    dk = B_KEY_DIM
    heads = [slice(g * LANES, (g + 1) * LANES) for g in range(group)]

    @pl.when(pl.program_id(2) == 0)
    def _():
        st_ref[...] = jnp.zeros(st_ref.shape, F32)

    lg = lbl_ref[...]
    e = jnp.exp(lg - jnp.max(lg, axis=0, keepdims=True))
    lb = e[0:1, :] / jnp.sum(e, axis=0, keepdims=True)
    ng = ng_ref[...]

    tri = (lax.broadcasted_iota(jnp.int32, (c_len, c_len), 0)
           >= lax.broadcasted_iota(jnp.int32, (c_len, c_len), 1)).astype(BF16)
    rowi = lax.broadcasted_iota(jnp.int32, (n_sub, sub, group * dk), 1)

    def chunk(c, carry):
        rows = pl.ds(pl.multiple_of(c * c_len, c_len), c_len)
        qb = q_ref[rows, :].astype(F32)
        fb = f_ref[rows, :].astype(F32)
        v = i_ref[rows, :].astype(F32)
        og = og_ref[rows, :].astype(F32)

        q = qb * _sigmoid(qb) * (dk ** -0.5)
        fg = lb + (1.0 - lb) * _sigmoid(fb)
        logf = jnp.log(fg)
        k = 1.0 - fg

        hi = logf.astype(BF16)
        r1 = logf - hi.astype(F32)
        mid = r1.astype(BF16)
        lo = (r1 - mid.astype(F32)).astype(BF16)
        b = _dot(tri, hi) + _dot(tri, mid) + _dot(tri, lo)

        states = [st_ref[g] for g in range(group)]
        qe = (q * jnp.exp(b)).astype(BF16)
        o_inter = [_dot_nt(qe[:, hs], states[g].astype(BF16)) for g, hs in enumerate(heads)]

        v16 = v.astype(BF16)
        pair_a = []
        for s_i in range(1, n_sub):
            lo_r = s_i * sub
            ref_b = b[lo_r - 1:lo_r, :]
            qt = (q[lo_r:lo_r + sub, :] * jnp.exp(b[lo_r:lo_r + sub, :] - ref_b)).astype(BF16)
            kt = (k[:lo_r, :] * jnp.exp(ref_b - b[:lo_r, :])).astype(BF16)
            pair_a.append([_dot_nt(qt[:, hs], kt[:, hs]).astype(BF16) for hs in heads])
        pair_o = [[_dot(pair_a[s_i - 1][g], v16[:s_i * sub, hs]) for s_i in range(1, n_sub)]
                  for g, hs in enumerate(heads)]

        b_last = b[c_len - 1:c_len, :]
        kl = (k * jnp.exp(b_last - b)).astype(BF16)
        decay = jnp.exp(b_last)
        for g, hs in enumerate(heads):
            st_ref[g] = states[g] * decay[:, hs] + _dot(v[:, hs].T.astype(BF16), kl[:, hs])

        b3 = b.reshape(n_sub, sub, group * dk)
        q3 = q.reshape(n_sub, sub, group * dk)
        k3 = k.reshape(n_sub, sub, group * dk)
        v3 = v.reshape(n_sub, sub, group * dk)
        od = [jnp.zeros((n_sub, sub, dk), F32) for _ in heads]
        for s_o in range(sub):
            dec = jnp.exp(jnp.minimum(b3 - b3[:, s_o:s_o + 1, :], 0.0))
            gq = jnp.where(rowi >= s_o, q3 * dec, 0.0) * k3[:, s_o:s_o + 1, :]
            for g, hs in enumerate(heads):
                od[g] = od[g] + jnp.sum(gq[:, :, hs], axis=-1, keepdims=True) * v3[:, s_o:s_o + 1, hs]

        gate = og * _sigmoid(og)
        for g, hs in enumerate(heads):
            o = o_inter[g] + jnp.concatenate([jnp.zeros((sub, dk), F32)] + pair_o[g], axis=0)
            o = o + od[g].reshape(c_len, dk)
            o = o * lax.rsqrt(jnp.mean(o * o, axis=-1, keepdims=True) + EPS) * ng[:, hs]
            o_ref[rows, hs] = (o * gate[:, hs]).astype(BF16)
        return carry

    lax.fori_loop(0, lt // c_len, chunk, 0)


def _hgrn(z, lb_logits, norm_g, batch, seq, lt, group):
    nt = seq // lt
    width = group * LANES
    d = B_HEADS * LANES
    seg = lambda off: pl.BlockSpec((lt, width), lambda b, hh, t: (b * nt + t, off * (d // width) + hh))
    return pl.pallas_call(
        functools.partial(_hgrn_kernel, lt=lt, group=group),
        grid=(batch, B_HEADS // group, nt),
        in_specs=[
            seg(1), seg(2), seg(3), seg(4),
            pl.BlockSpec((lb_logits.shape[0], width), lambda b, hh, t: (0, hh)),
            pl.BlockSpec((1, width), lambda b, hh, t: (0, hh)),
        ],
        out_specs=pl.BlockSpec((lt, width), lambda b, hh, t: (b * nt + t, hh)),
        out_shape=jax.ShapeDtypeStruct((batch * seq, d), BF16),
        scratch_shapes=[pltpu.VMEM((group, LANES, B_KEY_DIM), F32)],
        compiler_params=_params(("parallel", "parallel", "arbitrary")),
        name="hgrn",
    )(z, z, z, z, lb_logits, norm_g)


def _merge_kernel(x_ref, a_ref, hg_ref, ga_ref, gb_ref, wa_ref, wb_ref, wo_ref, gx_ref, wq_ref,
                  km_ref, vm_ref, wox_ref, o_ref):
    ya = _dot(a_ref[...], wa_ref[...])
    yb = _dot(hg_ref[...], wb_ref[...])
    mg = _sigmoid(ga_ref[...].astype(F32)) * ya + _sigmoid(gb_ref[...].astype(F32)) * yb
    h = x_ref[...] + _dot(mg.astype(BF16), wo_ref[...])

    hn = _rms(h, gx_ref[...]).astype(BF16)
    q = _dot(hn, wq_ref[...])
    d = q.shape[1]
    xd = d // X_HEADS
    outs = []
    for hh in range(X_HEADS):
        sl = slice(hh * xd, (hh + 1) * xd)
        s = _dot_nt(q[:, sl].astype(BF16), km_ref[:, sl]) * (xd ** -0.5)
        p = jnp.exp(s - jnp.max(s, axis=-1, keepdims=True))
        oh = _dot(p.astype(BF16), vm_ref[:, sl])
        outs.append(oh / jnp.sum(p, axis=-1, keepdims=True))
    o = jnp.concatenate(outs, axis=-1)
    o_ref[...] = h + _dot(o.astype(BF16), wox_ref[...])


def _merge(x2d, attn, hg, z, wa, wb, wo, gx, wq, kvm, wox, batch, seq, n_mem, tm):
    d = x2d.shape[1]
    nt = seq // tm
    gcol = (z.shape[1] - 2 * d) // d
    row = lambda b, t: (b * nt + t, 0)
    full = lambda b, t: (0, 0)
    wspec = pl.BlockSpec((d, d), full)
    return pl.pallas_call(
        _merge_kernel,
        grid=(batch, nt),
        in_specs=[
            pl.BlockSpec((tm, d), row),
            pl.BlockSpec((tm, d), row),
            pl.BlockSpec((tm, d), row),
            pl.BlockSpec((tm, d), lambda b, t: (b * nt + t, gcol)),
            pl.BlockSpec((tm, d), lambda b, t: (b * nt + t, gcol + 1)),
            wspec, wspec, wspec,
            pl.BlockSpec((1, d), full),
            wspec,
            pl.BlockSpec((n_mem, d), lambda b, t: (b, 0)),
            pl.BlockSpec((n_mem, d), lambda b, t: (b, 1)),
            wspec,
        ],
        out_specs=pl.BlockSpec((tm, d), row),
        out_shape=jax.ShapeDtypeStruct(x2d.shape, F32),
        compiler_params=_params(("parallel", "parallel")),
        name="merge",
    )(x2d, attn, hg, z, z, wa, wb, wo, gx, wq, kvm, kvm, wox)


def _ffn_kernel(h_ref, g_ref, wi_ref, wo_ref, gf_ref, o_ref, *, hidden, bounds):
    h = h_ref[...]
    hn = _rms(h, g_ref[...]).astype(BF16)
    acc = h
    for lo, hi in bounds:
        gate = _dot(hn, wi_ref[:, lo:hi])
        up = _dot(hn, wi_ref[:, hidden + lo:hidden + hi])
        act = (gate * _sigmoid(gate) * up).astype(BF16)
        acc = acc + _dot(act, wo_ref[lo:hi, :])
    o_ref[...] = _rms(acc, gf_ref[...])


def _ffn(h2d, g, wi, wo, gf, tm):
    t, d = h2d.shape
    hidden = wo.shape[0]
    step = 1024
    bounds = tuple((lo, min(lo + step, hidden)) for lo in range(0, hidden, step))
    full = lambda i: (0, 0)
    return pl.pallas_call(
        functools.partial(_ffn_kernel, hidden=hidden, bounds=bounds),
        grid=(t // tm,),
        in_specs=[
            pl.BlockSpec((tm, d), lambda i: (i, 0)),
            pl.BlockSpec((1, d), full),
            pl.BlockSpec(wi.shape, full, pipeline_mode=pl.Buffered(1)),
            pl.BlockSpec(wo.shape, full, pipeline_mode=pl.Buffered(1)),
            pl.BlockSpec((1, d), full),
        ],
        out_specs=pl.BlockSpec((tm, d), lambda i: (i, 0)),
        out_shape=jax.ShapeDtypeStruct(h2d.shape, F32),
        compiler_params=_params(("parallel",)),
        name="ffn",
    )(h2d, g, wi, wo, gf)


def _rope_tables(positions):
    inv_freq = jnp.power(ROPE_THETA, -jnp.arange(0, A_HEAD_DIM, 2, dtype=F32) / A_HEAD_DIM)
    ang = positions.astype(F32).reshape(-1, 1) * inv_freq
    cos, sin = jnp.cos(ang), jnp.sin(ang)
    reps = LANES // A_HEAD_DIM
    cos_t = jnp.tile(jnp.concatenate([cos, cos], axis=-1), (1, reps))
    sin_t = jnp.tile(jnp.concatenate([-sin, sin], axis=-1), (1, reps))
    return cos_t, sin_t, cos.T, sin.T


def kernel(x, mem, positions, norm_mix_g, w_in, hgrn_lb_logits, hgrn_norm_g, w_br_attn, w_br_hgrn, w_out,
           norm_x_g, norm_mem_g, wq_x, wkv_x, wo_x, norm_ffn_g, w_ffn_in, w_ffn_out, final_norm_g):
    batch, seq, d = x.shape
    n_mem = mem.shape[1]
    depth = w_in.shape[0]
    assert depth == 1 and d == A_HEADS * A_HEAD_DIM == B_HEADS * LANES and B_KEY_DIM == LANES
    tm = min(512, seq)
    tq = MOBA_QUERY_BLOCKS * MOBA_BLOCK
    assert seq % tq == 0 and seq % tm == 0 and w_in.shape[2] == 9 * d
    lt = min(1024, seq)
    x2d = x.reshape(batch * seq, d)
    cos_t, sin_t, cos_tt, sin_tt = _rope_tables(positions)
    row = lambda g: g.reshape(1, -1).astype(F32)
    bf = lambda w: w.astype(BF16)

    w_in16 = bf(w_in[0])
    wt_qv = jnp.stack([w_in16[:, :d].T, w_in16[:, 2 * d:3 * d].T])

    kvm = _memkv(mem.reshape(batch * n_mem, d), row(norm_mem_g[0]), bf(wkv_x[0]), n_mem)
    z, qt, vt = _inproj(x2d, row(norm_mix_g[0]), w_in16, wt_qv, cos_t, sin_t, cos_tt, sin_tt, tm)
    attn = _moba(z, qt, vt, batch, seq, tq)
    hg = _hgrn(z, hgrn_lb_logits.astype(F32), row(hgrn_norm_g[0]), batch, seq, lt, HGRN_GROUP)
    h = _merge(x2d, attn, hg, z, bf(w_br_attn[0]), bf(w_br_hgrn[0]), bf(w_out[0]), row(norm_x_g[0]),
               bf(wq_x[0]), kvm, bf(wo_x[0]), batch, seq, n_mem, tm)
    out = _ffn(h, row(norm_ffn_g[0]), bf(w_ffn_in[0]), bf(w_ffn_out[0]), row(final_norm_g), tm)
    return out.reshape(batch, seq, d)
```

```python
import functools

import jax
import jax.numpy as jnp
from jax import lax
from jax.experimental import pallas as pl
from jax.experimental.pallas import tpu as pltpu

F32 = jnp.float32
BF16 = jnp.bfloat16

EPS = 1e-6
NEG = -1e30
ROPE_THETA = 10000.0

A_HEADS = 16
A_HEAD_DIM = 64
MOBA_BLOCK = 256
MOBA_TOPK = 3
MOBA_QUERY_BLOCKS = 4
MOBA_LOOKAHEAD = 3
B_HEADS = 8
B_KEY_DIM = 128
X_HEADS = 4
HGRN_CHUNK = 64
HGRN_SUB = 16
HGRN_GROUP = 4

LANES = 128
SUM_ROWS = 16
LOG2_E = 1.4426950408889634
VMEM_LIMIT = 56 * 1024 * 1024


def _params(sem, flags=None):
    return pltpu.CompilerParams(dimension_semantics=sem, vmem_limit_bytes=VMEM_LIMIT, flags=flags)


def _rms(x, g):
    return x * lax.rsqrt(jnp.mean(x * x, axis=-1, keepdims=True) + EPS) * g


def _dot(a, b):
    return jnp.dot(a, b, preferred_element_type=F32)


def _dot_nt(a, b):
    return lax.dot_general(a, b, (((1,), (1,)), ((), ())), preferred_element_type=F32)


def _sigmoid(x):
    return 1.0 / (1.0 + jnp.exp(-x))


def _memkv_kernel(mem_ref, g_ref, w_ref, o_ref):
    y = _rms(mem_ref[...], g_ref[...])
    o_ref[...] = _dot(y.astype(BF16), w_ref[...]).astype(BF16)


def _memkv(mem2d, g, w, n_mem):
    rows, d = mem2d.shape
    return pl.pallas_call(
        _memkv_kernel,
        grid=(rows // n_mem,),
        in_specs=[
            pl.BlockSpec((n_mem, d), lambda b: (b, 0)),
            pl.BlockSpec((1, d), lambda b: (0, 0)),
            pl.BlockSpec(w.shape, lambda b: (0, 0)),
        ],
        out_specs=pl.BlockSpec((n_mem, w.shape[1]), lambda b: (b, 0)),
        out_shape=jax.ShapeDtypeStruct((rows, w.shape[1]), BF16),
        compiler_params=_params(("parallel",)),
        name="memkv",
    )(mem2d, g, w)


def _inproj_kernel(x_ref, g_ref, w_ref, wt_ref, cos_ref, sin_ref, cost_ref, sint_ref,
                   z_ref, qt_ref, vt_ref, xn_ref, *, tm):
    j = pl.program_id(1)
    hd, half = A_HEAD_DIM, A_HEAD_DIM // 2

    @pl.when(j == 0)
    def _():
        xn_ref[...] = _rms(x_ref[...], g_ref[...]).astype(BF16)
        acc = _dot_nt(wt_ref[...], xn_ref[...])
        cos = cost_ref[...]
        sin = sint_ref[...]
        scale = hd ** -0.5 * LOG2_E
        for h in range(acc.shape[0] // hd):
            t1 = acc[h * hd:h * hd + half, :]
            t2 = acc[h * hd + half:(h + 1) * hd, :]
            qt_ref[h * hd:h * hd + half, :] = ((t1 * cos - t2 * sin) * scale).astype(BF16)
            qt_ref[h * hd + half:(h + 1) * hd, :] = ((t2 * cos + t1 * sin) * scale).astype(BF16)

    @pl.when(j == 1)
    def _():
        acc = _dot(xn_ref[...], w_ref[...])
        cos = cos_ref[...]
        sin = sin_ref[...]
        lane = lax.broadcasted_iota(jnp.int32, cos.shape, 1)
        low = (lane % hd) < half
        for c in range(acc.shape[1] // LANES):
            t = acc[:, c * LANES:(c + 1) * LANES]
            rot = jnp.where(low, pltpu.roll(t, LANES - half, 1), pltpu.roll(t, half, 1))
            z_ref[:, c * LANES:(c + 1) * LANES] = (t * cos + rot * sin).astype(BF16)

    @pl.when(j == 2)
    def _():
        acc = _dot_nt(wt_ref[...], xn_ref[...]).astype(BF16)
        for n in range(tm // MOBA_BLOCK):
            vt_ref[n] = acc[:, n * MOBA_BLOCK:(n + 1) * MOBA_BLOCK]

    @pl.when(j >= 3)
    def _():
        z_ref[...] = _dot(xn_ref[...], w_ref[...]).astype(BF16)


def _inproj(x2d, g, w, wt, cos_t, sin_t, cos_tt, sin_tt, tm):
    t, d = x2d.shape
    nseg = w.shape[1] // d
    blk = MOBA_BLOCK
    w_col = lambda i, j: (0, jnp.where(j == 0, 1, jnp.where(j == 2, 3, j)))
    z_col = lambda i, j: (i, jnp.where(j <= 2, 0, j - 2))
    return pl.pallas_call(
        functools.partial(_inproj_kernel, tm=tm),
        grid=(t // tm, nseg),
        in_specs=[
            pl.BlockSpec((tm, d), lambda i, j: (i, 0)),
            pl.BlockSpec((1, d), lambda i, j: (0, 0)),
            pl.BlockSpec((d, d), w_col),
            pl.BlockSpec((None, d, d), lambda i, j: (jnp.where(j < 2, 0, 1), 0, 0)),
            pl.BlockSpec((tm, LANES), lambda i, j: (i, 0)),
            pl.BlockSpec((tm, LANES), lambda i, j: (i, 0)),
            pl.BlockSpec((A_HEAD_DIM // 2, tm), lambda i, j: (0, i)),
            pl.BlockSpec((A_HEAD_DIM // 2, tm), lambda i, j: (0, i)),
        ],
        out_specs=[
            pl.BlockSpec((tm, d), z_col),
            pl.BlockSpec((d, tm), lambda i, j: (0, i)),
            pl.BlockSpec((tm // blk, d, blk), lambda i, j: (i, 0, 0)),
        ],
        out_shape=[
            jax.ShapeDtypeStruct((t, (nseg - 2) * d), BF16),
            jax.ShapeDtypeStruct((d, t), BF16),
            jax.ShapeDtypeStruct((t // blk, d, blk), BF16),
        ],
        scratch_shapes=[pltpu.VMEM((tm, d), BF16)],
        compiler_params=_params(("parallel", "arbitrary")),
        name="inproj",
    )(x2d, g, w, wt, cos_t, sin_t, cos_tt, sin_tt)


def _moba_kernel(q_ref, k_ref, v_ref, o_ref, km_ref, sel_ref, m_ref, acc_ref, s_ref, *, nb, blk, tq):
    i = pl.program_id(2)
    hd = A_HEAD_DIM
    halves = tq // blk

    @pl.when(i == 0)
    def _prep():
        for n in range(nb):
            km_ref[n:n + 1, :] = jnp.mean(k_ref[n * blk:(n + 1) * blk, :].astype(F32), axis=0, keepdims=True)

    q_t = q_ref[...]
    row = lax.broadcasted_iota(jnp.int32, q_t.shape, 0)
    zero = jnp.zeros_like(q_t)
    qz = [jnp.where(row < hd, q_t, zero), jnp.where(row >= hd, q_t, zero)]

    km = km_ref[...]
    km_hi = km.astype(BF16)
    km_lo = (km - km_hi.astype(F32)).astype(BF16)
    blk_id = lax.broadcasted_iota(jnp.int32, (nb, tq), 0)
    q_blk = i * halves + lax.broadcasted_iota(jnp.int32, (nb, tq), 1) // blk
    for h in range(2):
        g = _dot(km_hi, qz[h]) + _dot(km_lo, qz[h])
        g = jnp.where(blk_id < q_blk, g, -jnp.inf)
        sel = jnp.zeros((nb, tq), F32)
        for _ in range(MOBA_TOPK):
            mx = jnp.max(g, axis=0, keepdims=True)
            first = jnp.min(jnp.where(g == mx, blk_id, nb), axis=0, keepdims=True)
            pick = jnp.logical_and(blk_id == first, mx > -jnp.inf)
            sel = jnp.where(pick, 1.0, sel)
            g = jnp.where(blk_id == first, -jnp.inf, g)
        sel_ref[h] = sel

    m_ref[...] = jnp.full(m_ref.shape, NEG, F32)
    acc_ref[...] = jnp.zeros(acc_ref.shape, F32)
    ones = jnp.ones((SUM_ROWS, blk), BF16)

    causal = (lax.broadcasted_iota(jnp.int32, (blk, blk), 0)
              <= lax.broadcasted_iota(jnp.int32, (blk, blk), 1))

    groups = [(h, c) for h in range(2) for c in range(halves)]

    def score(slot, j, g):
        def run():
            h, c = groups[g]
            kj = k_ref[pl.ds(pl.multiple_of(j * blk, blk), blk), :]
            s_ref[slot, g] = _dot(kj, qz[h][:, c * blk:(c + 1) * blk])
        return run

    def finish(slot, j, g, own):
        def run():
            h, c = groups[g]
            cs = slice(c * blk, (c + 1) * blk)
            s = s_ref[slot, g]
            m_old = m_ref[h:h + 1, cs]
            if own:
                s = jnp.where(causal, s, NEG)
                m_new = jnp.maximum(m_old, jnp.max(s, axis=0, keepdims=True))
                m_use = m_new
            else:
                on = sel_ref[h, pl.ds(j, 1), cs] > 0.0
                m_new = jnp.where(on, jnp.maximum(m_old, jnp.max(s, axis=0, keepdims=True)), m_old)
                m_use = jnp.where(on, m_new, -NEG)
            p = jnp.exp2(s - m_use).astype(BF16)
            pv = _dot(jnp.concatenate([v_ref[j, h * hd:(h + 1) * hd, :], ones], axis=0), p)
            acc_ref[h, :, cs] = jnp.exp2(m_old - m_new) * acc_ref[h, :, cs] + pv
            m_ref[h:h + 1, cs] = m_new
        return run

    def overlap(scores, finishes):
        for run in scores[:MOBA_LOOKAHEAD]:
            run()
        rest = scores[MOBA_LOOKAHEAD:]
        for run in finishes:
            run()
            if rest:
                rest.pop(0)()
        for run in rest:
            run()

    everyone = range(len(groups))
    overlap([score(0, 0, g) for g in everyone], [])

    def body(jj, carry):
        j0 = 2 * jj
        overlap([score(1, j0 + 1, g) for g in everyone], [finish(0, j0, g, False) for g in everyone])
        overlap([score(0, j0 + 2, g) for g in everyone], [finish(1, j0 + 1, g, False) for g in everyone])
        return carry

    lax.fori_loop(0, i * (halves // 2), body, 0)

    base = i * halves
    for t in range(halves):
        later = [g for g in everyone if groups[g][1] > t]
        overlap([score((t + 1) % 2, base + t + 1, g) for g in later],
                [finish(t % 2, base + t, g, groups[g][1] == t) for g in everyone if groups[g][1] >= t])

    out_t = jnp.concatenate([acc_ref[h, :hd, :] / acc_ref[h, hd:hd + 1, :] for h in range(2)], axis=0)
    o_ref[...] = out_t.T.astype(BF16)


def _moba(z, qt, vt, batch, seq, tq):
    blk = MOBA_BLOCK
    nb = seq // blk
    nq = seq // tq
    d = qt.shape[0]
    pairs = d // LANES
    return pl.pallas_call(
        functools.partial(_moba_kernel, nb=nb, blk=blk, tq=tq),
        grid=(batch, pairs, nq),
        in_specs=[
            pl.BlockSpec((LANES, tq), lambda b, p, i: (p, b * nq + i)),
            pl.BlockSpec((seq, LANES), lambda b, p, i: (b, p)),
            pl.BlockSpec((nb, LANES, blk), lambda b, p, i: (b, p, 0)),
        ],
        out_specs=pl.BlockSpec((tq, LANES), lambda b, p, i: (b * nq + i, p)),
        out_shape=jax.ShapeDtypeStruct((batch * seq, d), BF16),
        scratch_shapes=[
            pltpu.VMEM((nb, LANES), F32),
            pltpu.VMEM((2, nb, tq), F32),
            pltpu.VMEM((2, tq), F32),
            pltpu.VMEM((2, A_HEAD_DIM + SUM_ROWS, tq), F32),
            pltpu.VMEM((2, 2 * (tq // blk), blk, blk), F32),
        ],
        compiler_params=_params(("parallel", "parallel", "arbitrary")),
        name="moba",
    )(qt, z, vt)


def _hgrn_kernel(q_ref, f_ref, i_ref, og_ref, lbl_ref, ng_ref, o_ref, st_ref, *, lt, group):
    c_len, sub = HGRN_CHUNK, HGRN_SUB
    n_sub = c_len // sub
    dk = B_KEY_DIM
    heads = [slice(g * LANES, (g + 1) * LANES) for g in range(group)]

    @pl.when(pl.program_id(2) == 0)
    def _():
        st_ref[...] = jnp.zeros(st_ref.shape, F32)

    lg = lbl_ref[...]
    e = jnp.exp(lg - jnp.max(lg, axis=0, keepdims=True))
    lb = e[0:1, :] / jnp.sum(e, axis=0, keepdims=True)
    ng = ng_ref[...]

    tri = (lax.broadcasted_iota(jnp.int32, (c_len, c_len), 0)
           >= lax.broadcasted_iota(jnp.int32, (c_len, c_len), 1)).astype(BF16)
    width = group * dk
    half = sub // 2
    n_half = c_len // half
    rowi = lax.broadcasted_iota(jnp.int32, (n_half, half, width), 1)
    same_sub = (lax.broadcasted_iota(jnp.int32, (n_sub * half, n_sub * half), 0) // half
                == lax.broadcasted_iota(jnp.int32, (n_sub * half, n_sub * half), 1) // half)

    def chunk(c, carry):
        rows = pl.ds(pl.multiple_of(c * c_len, c_len), c_len)
        qb = q_ref[rows, :].astype(F32)
        fb = f_ref[rows, :].astype(F32)
        v = i_ref[rows, :].astype(F32)
        og = og_ref[rows, :].astype(F32)

        q = qb * _sigmoid(qb) * (dk ** -0.5)
        fg = lb + (1.0 - lb) * _sigmoid(fb)
        logf = jnp.log2(fg)
        k = 1.0 - fg

        hi = logf.astype(BF16)
        r1 = logf - hi.astype(F32)
        mid = r1.astype(BF16)
        lo = (r1 - mid.astype(F32)).astype(BF16)
        b = _dot(tri, hi) + _dot(tri, mid) + _dot(tri, lo)

        states = [st_ref[g] for g in range(group)]
        qe = (q * jnp.exp2(b)).astype(BF16)
        o_inter = [_dot_nt(qe[:, hs], states[g].astype(BF16)) for g, hs in enumerate(heads)]

        v16 = v.astype(BF16)
        pair_a = []
        for s_i in range(1, n_sub):
            lo_r = s_i * sub
            ref_b = b[lo_r - 1:lo_r, :]
            qt = (q[lo_r:lo_r + sub, :] * jnp.exp2(b[lo_r:lo_r + sub, :] - ref_b)).astype(BF16)
            kt = (k[:lo_r, :] * jnp.exp2(ref_b - b[:lo_r, :])).astype(BF16)
            pair_a.append([_dot_nt(qt[:, hs], kt[:, hs]).astype(BF16) for hs in heads])
        pair_o = [[_dot(pair_a[s_i - 1][g], v16[:s_i * sub, hs]) for s_i in range(1, n_sub)]
                  for g, hs in enumerate(heads)]

        b3 = b.reshape(n_sub, sub, width)
        ref_h = b3[:, half - 1:half, :]
        q_h = q.reshape(n_sub, sub, width)[:, half:, :] * jnp.exp2(b3[:, half:, :] - ref_h)
        k_h = k.reshape(n_sub, sub, width)[:, :half, :] * jnp.exp2(ref_h - b3[:, :half, :])
        q_h = q_h.reshape(n_sub * half, width).astype(BF16)
        k_h = k_h.reshape(n_sub * half, width).astype(BF16)
        v_h = v.reshape(n_sub, sub, width)[:, :half, :].reshape(n_sub * half, width).astype(BF16)
        half_a = [jnp.where(same_sub, _dot_nt(q_h[:, hs], k_h[:, hs]), 0.0).astype(BF16) for hs in heads]
        half_o = [_dot(half_a[g], v_h[:, hs]) for g, hs in enumerate(heads)]

        b_last = b[c_len - 1:c_len, :]
        kl = (k * jnp.exp2(b_last - b)).astype(BF16)
        decay = jnp.exp2(b_last)
        for g, hs in enumerate(heads):
            st_ref[g] = states[g] * decay[:, hs] + _dot(v[:, hs].T.astype(BF16), kl[:, hs])

        b8 = b.reshape(n_half, half, width)
        q8 = q.reshape(n_half, half, width)
        k8 = k.reshape(n_half, half, width)
        v8 = v.reshape(n_half, half, width)
        od = [jnp.zeros((n_half, half, dk), F32) for _ in heads]
        for s_o in range(half):
            dec = jnp.exp2(jnp.where(rowi >= s_o, b8 - b8[:, s_o:s_o + 1, :], NEG))
            gq = q8 * dec * k8[:, s_o:s_o + 1, :]
            for g, hs in enumerate(heads):
                od[g] = od[g] + jnp.sum(gq[:, :, hs], axis=-1, keepdims=True) * v8[:, s_o:s_o + 1, hs]

        gate = og * _sigmoid(og)
        for g, hs in enumerate(heads):
            o = o_inter[g] + jnp.concatenate([jnp.zeros((sub, dk), F32)] + pair_o[g], axis=0)
            second = jnp.concatenate([jnp.zeros((n_sub, half, dk), F32), half_o[g].reshape(n_sub, half, dk)], axis=1)
            o = o + second.reshape(c_len, dk) + od[g].reshape(c_len, dk)
            o = o * lax.rsqrt(jnp.mean(o * o, axis=-1, keepdims=True) + EPS) * ng[:, hs]
            o_ref[rows, hs] = (o * gate[:, hs]).astype(BF16)
        return carry

    lax.fori_loop(0, lt // c_len, chunk, 0)


def _hgrn(z, lb_logits, norm_g, batch, seq, lt, group):
    nt = seq // lt
    width = group * LANES
    d = B_HEADS * LANES
    seg = lambda off: pl.BlockSpec((lt, width), lambda b, hh, t: (b * nt + t, off * (d // width) + hh))
    return pl.pallas_call(
        functools.partial(_hgrn_kernel, lt=lt, group=group),
        grid=(batch, B_HEADS // group, nt),
        in_specs=[
            seg(1), seg(2), seg(3), seg(4),
            pl.BlockSpec((lb_logits.shape[0], width), lambda b, hh, t: (0, hh)),
            pl.BlockSpec((1, width), lambda b, hh, t: (0, hh)),
        ],
        out_specs=pl.BlockSpec((lt, width), lambda b, hh, t: (b * nt + t, hh)),
        out_shape=jax.ShapeDtypeStruct((batch * seq, d), BF16),
        scratch_shapes=[pltpu.VMEM((group, LANES, B_KEY_DIM), F32)],
        compiler_params=_params(("parallel", "parallel", "arbitrary")),
        name="hgrn",
    )(z, z, z, z, lb_logits, norm_g)


def _merge_kernel(x_ref, a_ref, hg_ref, ga_ref, gb_ref, wa_ref, wb_ref, wo_ref, gx_ref, wq_ref,
                  km_ref, vm_ref, wox_ref, o_ref):
    ya = _dot(a_ref[...], wa_ref[...])
    yb = _dot(hg_ref[...], wb_ref[...])
    mg = _sigmoid(ga_ref[...].astype(F32)) * ya + _sigmoid(gb_ref[...].astype(F32)) * yb
    h = x_ref[...] + _dot(mg.astype(BF16), wo_ref[...])

    hn = _rms(h, gx_ref[...]).astype(BF16)
    q = _dot(hn, wq_ref[...])
    d = q.shape[1]
    xd = d // X_HEADS
    outs = []
    for hh in range(X_HEADS):
        sl = slice(hh * xd, (hh + 1) * xd)
        s = _dot_nt(q[:, sl].astype(BF16), km_ref[:, sl]) * (xd ** -0.5)
        p = jnp.exp(s - jnp.max(s, axis=-1, keepdims=True))
        oh = _dot(p.astype(BF16), vm_ref[:, sl])
        outs.append(oh / jnp.sum(p, axis=-1, keepdims=True))
    o = jnp.concatenate(outs, axis=-1)
    o_ref[...] = h + _dot(o.astype(BF16), wox_ref[...])


def _merge(x2d, attn, hg, z, wa, wb, wo, gx, wq, kvm, wox, batch, seq, n_mem, tm):
    d = x2d.shape[1]
    nt = seq // tm
    gcol = (z.shape[1] - 2 * d) // d
    row = lambda b, t: (b * nt + t, 0)
    full = lambda b, t: (0, 0)
    wspec = pl.BlockSpec((d, d), full)
    return pl.pallas_call(
        _merge_kernel,
        grid=(batch, nt),
        in_specs=[
            pl.BlockSpec((tm, d), row),
            pl.BlockSpec((tm, d), row),
            pl.BlockSpec((tm, d), row),
            pl.BlockSpec((tm, d), lambda b, t: (b * nt + t, gcol)),
            pl.BlockSpec((tm, d), lambda b, t: (b * nt + t, gcol + 1)),
            wspec, wspec, wspec,
            pl.BlockSpec((1, d), full),
            wspec,
            pl.BlockSpec((n_mem, d), lambda b, t: (b, 0)),
            pl.BlockSpec((n_mem, d), lambda b, t: (b, 1)),
            wspec,
        ],
        out_specs=pl.BlockSpec((tm, d), row),
        out_shape=jax.ShapeDtypeStruct(x2d.shape, F32),
        compiler_params=_params(("parallel", "parallel")),
        name="merge",
    )(x2d, attn, hg, z, z, wa, wb, wo, gx, wq, kvm, kvm, wox)


def _ffn_kernel(h_ref, g_ref, wi_ref, wo_ref, gf_ref, o_ref, *, hidden, bounds):
    h = h_ref[...]
    hn = _rms(h, g_ref[...]).astype(BF16)
    acc = h
    for lo, hi in bounds:
        gate = _dot(hn, wi_ref[:, lo:hi])
        up = _dot(hn, wi_ref[:, hidden + lo:hidden + hi])
        act = (gate * _sigmoid(gate) * up).astype(BF16)
        acc = acc + _dot(act, wo_ref[lo:hi, :])
    o_ref[...] = _rms(acc, gf_ref[...])


def _ffn(h2d, g, wi, wo, gf, tm):
    t, d = h2d.shape
    hidden = wo.shape[0]
    step = 1024
    bounds = tuple((lo, min(lo + step, hidden)) for lo in range(0, hidden, step))
    full = lambda i: (0, 0)
    return pl.pallas_call(
        functools.partial(_ffn_kernel, hidden=hidden, bounds=bounds),
        grid=(t // tm,),
        in_specs=[
            pl.BlockSpec((tm, d), lambda i: (i, 0)),
            pl.BlockSpec((1, d), full),
            pl.BlockSpec(wi.shape, full, pipeline_mode=pl.Buffered(1)),
            pl.BlockSpec(wo.shape, full, pipeline_mode=pl.Buffered(1)),
            pl.BlockSpec((1, d), full),
        ],
        out_specs=pl.BlockSpec((tm, d), lambda i: (i, 0)),
        out_shape=jax.ShapeDtypeStruct(h2d.shape, F32),
        compiler_params=_params(("parallel",)),
        name="ffn",
    )(h2d, g, wi, wo, gf)


def _rope_tables(positions):
    inv_freq = jnp.power(ROPE_THETA, -jnp.arange(0, A_HEAD_DIM, 2, dtype=F32) / A_HEAD_DIM)
    ang = positions.astype(F32).reshape(-1, 1) * inv_freq
    cos, sin = jnp.cos(ang), jnp.sin(ang)
    reps = LANES // A_HEAD_DIM
    cos_t = jnp.tile(jnp.concatenate([cos, cos], axis=-1), (1, reps))
    sin_t = jnp.tile(jnp.concatenate([-sin, sin], axis=-1), (1, reps))
    return cos_t, sin_t, cos.T, sin.T


def kernel(x, mem, positions, norm_mix_g, w_in, hgrn_lb_logits, hgrn_norm_g, w_br_attn, w_br_hgrn, w_out,
           norm_x_g, norm_mem_g, wq_x, wkv_x, wo_x, norm_ffn_g, w_ffn_in, w_ffn_out, final_norm_g):
    batch, seq, d = x.shape
    n_mem = mem.shape[1]
    depth = w_in.shape[0]
    assert depth == 1 and d == A_HEADS * A_HEAD_DIM == B_HEADS * LANES and B_KEY_DIM == LANES
    tm = min(512, seq)
    tm_in = min(1024, seq)
    tq = MOBA_QUERY_BLOCKS * MOBA_BLOCK
    assert seq % tq == 0 and seq % tm == 0 and w_in.shape[2] == 9 * d
    lt = min(1024, seq)
    x2d = x.reshape(batch * seq, d)
    cos_t, sin_t, cos_tt, sin_tt = _rope_tables(positions)
    row = lambda g: g.reshape(1, -1).astype(F32)
    bf = lambda w: w.astype(BF16)

    w_in16 = bf(w_in[0])
    wt_qv = jnp.stack([w_in16[:, :d].T, w_in16[:, 2 * d:3 * d].T])

    kvm = _memkv(mem.reshape(batch * n_mem, d), row(norm_mem_g[0]), bf(wkv_x[0]), n_mem)
    z, qt, vt = _inproj(x2d, row(norm_mix_g[0]), w_in16, wt_qv, cos_t, sin_t, cos_tt, sin_tt, tm_in)
    attn = _moba(z, qt, vt, batch, seq, tq)
    hg = _hgrn(z, hgrn_lb_logits.astype(F32), row(hgrn_norm_g[0]), batch, seq, lt, HGRN_GROUP)
    h = _merge(x2d, attn, hg, z, bf(w_br_attn[0]), bf(w_br_hgrn[0]), bf(w_out[0]), row(norm_x_g[0]),
               bf(wq_x[0]), kvm, bf(wo_x[0]), batch, seq, n_mem, tm)
    out = _ffn(h, row(norm_ffn_g[0]), bf(w_ffn_in[0]), bf(w_ffn_out[0]), row(final_norm_g), tm)
    return out.reshape(batch, seq, d)
```

```python
import functools

import jax
import jax.numpy as jnp
from jax import lax
from jax.experimental import pallas as pl
from jax.experimental.pallas import tpu as pltpu

F32 = jnp.float32
BF16 = jnp.bfloat16

EPS = 1e-6
NEG = -1e30
ROPE_THETA = 10000.0

A_HEADS = 16
A_HEAD_DIM = 64
MOBA_BLOCK = 256
MOBA_TOPK = 3
MOBA_QUERY_BLOCKS = 4
MOBA_LOOKAHEAD = 3
B_HEADS = 8
B_KEY_DIM = 128
X_HEADS = 4
HGRN_CHUNK = 64
HGRN_SUB = 16
HGRN_GROUP = 4

LANES = 128
SUM_ROWS = 16
LOG2_E = 1.4426950408889634
VMEM_LIMIT = 56 * 1024 * 1024


def _params(sem, flags=None):
    return pltpu.CompilerParams(dimension_semantics=sem, vmem_limit_bytes=VMEM_LIMIT, flags=flags)


def _rms(x, g):
    return x * lax.rsqrt(jnp.mean(x * x, axis=-1, keepdims=True) + EPS) * g


def _dot(a, b):
    return jnp.dot(a, b, preferred_element_type=F32)


def _dot_nt(a, b):
    return lax.dot_general(a, b, (((1,), (1,)), ((), ())), preferred_element_type=F32)


def _sigmoid(x):
    return 1.0 / (1.0 + jnp.exp(-x))


def _memkv_kernel(mem_ref, g_ref, w_ref, o_ref):
    y = _rms(mem_ref[...], g_ref[...])
    o_ref[...] = _dot(y.astype(BF16), w_ref[...]).astype(BF16)


def _memkv(mem2d, g, w, n_mem):
    rows, d = mem2d.shape
    return pl.pallas_call(
        _memkv_kernel,
        grid=(rows // n_mem,),
        in_specs=[
            pl.BlockSpec((n_mem, d), lambda b: (b, 0)),
            pl.BlockSpec((1, d), lambda b: (0, 0)),
            pl.BlockSpec(w.shape, lambda b: (0, 0)),
        ],
        out_specs=pl.BlockSpec((n_mem, w.shape[1]), lambda b: (b, 0)),
        out_shape=jax.ShapeDtypeStruct((rows, w.shape[1]), BF16),
        compiler_params=_params(("parallel",)),
        name="memkv",
    )(mem2d, g, w)


def _inproj_kernel(x_ref, g_ref, w_ref, wt_ref, cos_ref, sin_ref, cost_ref, sint_ref, z_ref, qt_ref, vt_ref, *, tm):
    hd, half = A_HEAD_DIM, A_HEAD_DIM // 2
    d = x_ref.shape[1]
    xn = _rms(x_ref[...], g_ref[...]).astype(BF16)

    acc = _dot_nt(wt_ref[0], xn)
    cos = cost_ref[...]
    sin = sint_ref[...]
    scale = hd ** -0.5 * LOG2_E
    for h in range(d // hd):
        t1 = acc[h * hd:h * hd + half, :]
        t2 = acc[h * hd + half:(h + 1) * hd, :]
        qt_ref[h * hd:h * hd + half, :] = ((t1 * cos - t2 * sin) * scale).astype(BF16)
        qt_ref[h * hd + half:(h + 1) * hd, :] = ((t2 * cos + t1 * sin) * scale).astype(BF16)

    acc = _dot(xn, w_ref[:, :d])
    cos = cos_ref[...]
    sin = sin_ref[...]
    lane = lax.broadcasted_iota(jnp.int32, cos.shape, 1)
    low = (lane % hd) < half
    for c in range(d // LANES):
        t = acc[:, c * LANES:(c + 1) * LANES]
        rot = jnp.where(low, pltpu.roll(t, LANES - half, 1), pltpu.roll(t, half, 1))
        z_ref[:, c * LANES:(c + 1) * LANES] = (t * cos + rot * sin).astype(BF16)

    acc = _dot_nt(wt_ref[1], xn).astype(BF16)
    for n in range(tm // MOBA_BLOCK):
        vt_ref[n] = acc[:, n * MOBA_BLOCK:(n + 1) * MOBA_BLOCK]

    for s in range(1, w_ref.shape[1] // d):
        z_ref[:, s * d:(s + 1) * d] = _dot(xn, w_ref[:, s * d:(s + 1) * d]).astype(BF16)


def _inproj(x2d, g, w, wt, cos_t, sin_t, cos_tt, sin_tt, tm):
    t, d = x2d.shape
    blk = MOBA_BLOCK
    full = lambda i: (0, 0)
    return pl.pallas_call(
        functools.partial(_inproj_kernel, tm=tm),
        grid=(t // tm,),
        in_specs=[
            pl.BlockSpec((tm, d), lambda i: (i, 0)),
            pl.BlockSpec((1, d), full),
            pl.BlockSpec(w.shape, full, pipeline_mode=pl.Buffered(1)),
            pl.BlockSpec(wt.shape, lambda i: (0, 0, 0), pipeline_mode=pl.Buffered(1)),
            pl.BlockSpec((tm, LANES), lambda i: (i, 0)),
            pl.BlockSpec((tm, LANES), lambda i: (i, 0)),
            pl.BlockSpec((A_HEAD_DIM // 2, tm), lambda i: (0, i)),
            pl.BlockSpec((A_HEAD_DIM // 2, tm), lambda i: (0, i)),
        ],
        out_specs=[
            pl.BlockSpec((tm, w.shape[1]), lambda i: (i, 0)),
            pl.BlockSpec((d, tm), lambda i: (0, i)),
            pl.BlockSpec((tm // blk, d, blk), lambda i: (i, 0, 0)),
        ],
        out_shape=[
            jax.ShapeDtypeStruct((t, w.shape[1]), BF16),
            jax.ShapeDtypeStruct((d, t), BF16),
            jax.ShapeDtypeStruct((t // blk, d, blk), BF16),
        ],
        compiler_params=_params(("parallel",)),
        name="inproj",
    )(x2d, g, w, wt, cos_t, sin_t, cos_tt, sin_tt)


def _moba_kernel(q_ref, k_ref, v_ref, o_ref, km_ref, sel_ref, m_ref, acc_ref, s_ref, *, nb, blk, tq):
    i = pl.program_id(2)
    hd = A_HEAD_DIM
    halves = tq // blk

    @pl.when(i == 0)
    def _prep():
        for n in range(nb):
            km_ref[n:n + 1, :] = jnp.mean(k_ref[n * blk:(n + 1) * blk, :].astype(F32), axis=0, keepdims=True)

    q_t = q_ref[...]
    row = lax.broadcasted_iota(jnp.int32, q_t.shape, 0)
    zero = jnp.zeros_like(q_t)
    qz = [jnp.where(row < hd, q_t, zero), jnp.where(row >= hd, q_t, zero)]

    km = km_ref[...]
    km_hi = km.astype(BF16)
    km_lo = (km - km_hi.astype(F32)).astype(BF16)
    blk_id = lax.broadcasted_iota(jnp.int32, (nb, tq), 0)
    q_blk = i * halves + lax.broadcasted_iota(jnp.int32, (nb, tq), 1) // blk
    gates = [_dot(km_hi, qz[h]) + _dot(km_lo, qz[h]) for h in range(2)]

    def select(h):
        def run():
            g = jnp.where(blk_id < q_blk, gates[h], -jnp.inf)
            sel = jnp.zeros((nb, tq), F32)
            for _ in range(MOBA_TOPK):
                mx = jnp.max(g, axis=0, keepdims=True)
                first = jnp.min(jnp.where(g == mx, blk_id, nb), axis=0, keepdims=True)
                pick = jnp.logical_and(blk_id == first, mx > -jnp.inf)
                sel = jnp.where(pick, 1.0, sel)
                g = jnp.where(blk_id == first, -jnp.inf, g)
            sel_ref[h] = sel
        return run

    m_ref[...] = jnp.full(m_ref.shape, NEG, F32)
    acc_ref[...] = jnp.zeros(acc_ref.shape, F32)
    ones = jnp.ones((SUM_ROWS, blk), BF16)

    causal = (lax.broadcasted_iota(jnp.int32, (blk, blk), 0)
              <= lax.broadcasted_iota(jnp.int32, (blk, blk), 1))

    groups = [(h, c) for h in range(2) for c in range(halves)]

    def score(slot, j, g):
        def run():
            h, c = groups[g]
            kj = k_ref[pl.ds(pl.multiple_of(j * blk, blk), blk), :]
            s_ref[slot, g] = _dot(kj, qz[h][:, c * blk:(c + 1) * blk])
        return run

    def finish(slot, j, g, own):
        def run():
            h, c = groups[g]
            cs = slice(c * blk, (c + 1) * blk)
            s = s_ref[slot, g]
            m_old = m_ref[h:h + 1, cs]
            if own:
                s = jnp.where(causal, s, NEG)
                m_new = jnp.maximum(m_old, jnp.max(s, axis=0, keepdims=True))
                m_use = m_new
            else:
                on = sel_ref[h, pl.ds(j, 1), cs] > 0.0
                m_new = jnp.where(on, jnp.maximum(m_old, jnp.max(s, axis=0, keepdims=True)), m_old)
                m_use = jnp.where(on, m_new, -NEG)
            p = jnp.exp2(s - m_use).astype(BF16)
            pv = _dot(jnp.concatenate([v_ref[j, h * hd:(h + 1) * hd, :], ones], axis=0), p)
            acc_ref[h, :, cs] = jnp.exp2(m_old - m_new) * acc_ref[h, :, cs] + pv
            m_ref[h:h + 1, cs] = m_new
        return run

    def overlap(scores, finishes):
        for run in scores[:MOBA_LOOKAHEAD]:
            run()
        rest = scores[MOBA_LOOKAHEAD:]
        for run in finishes:
            run()
            if rest:
                rest.pop(0)()
        for run in rest:
            run()

    everyone = range(len(groups))
    overlap([score(0, 0, g) for g in everyone], [select(0), select(1)])

    def body(jj, carry):
        j0 = 2 * jj
        overlap([score(1, j0 + 1, g) for g in everyone], [finish(0, j0, g, False) for g in everyone])
        overlap([score(0, j0 + 2, g) for g in everyone], [finish(1, j0 + 1, g, False) for g in everyone])
        return carry

    lax.fori_loop(0, i * (halves // 2), body, 0)

    base = i * halves
    for t in range(halves):
        later = [g for g in everyone if groups[g][1] > t]
        overlap([score((t + 1) % 2, base + t + 1, g) for g in later],
                [finish(t % 2, base + t, g, groups[g][1] == t) for g in everyone if groups[g][1] >= t])

    out_t = jnp.concatenate([acc_ref[h, :hd, :] / acc_ref[h, hd:hd + 1, :] for h in range(2)], axis=0)
    o_ref[...] = out_t.T.astype(BF16)


def _moba(z, qt, vt, batch, seq, tq):
    blk = MOBA_BLOCK
    nb = seq // blk
    nq = seq // tq
    d = qt.shape[0]
    pairs = d // LANES
    return pl.pallas_call(
        functools.partial(_moba_kernel, nb=nb, blk=blk, tq=tq),
        grid=(batch, pairs, nq),
        in_specs=[
            pl.BlockSpec((LANES, tq), lambda b, p, i: (p, b * nq + i)),
            pl.BlockSpec((seq, LANES), lambda b, p, i: (b, p)),
            pl.BlockSpec((nb, LANES, blk), lambda b, p, i: (b, p, 0)),
        ],
        out_specs=pl.BlockSpec((tq, LANES), lambda b, p, i: (b * nq + i, p)),
        out_shape=jax.ShapeDtypeStruct((batch * seq, d), BF16),
        scratch_shapes=[
            pltpu.VMEM((nb, LANES), F32),
            pltpu.VMEM((2, nb, tq), F32),
            pltpu.VMEM((2, tq), F32),
            pltpu.VMEM((2, A_HEAD_DIM + SUM_ROWS, tq), F32),
            pltpu.VMEM((2, 2 * (tq // blk), blk, blk), F32),
        ],
        compiler_params=_params(("parallel", "parallel", "arbitrary")),
        name="moba",
    )(qt, z, vt)


def _hgrn_kernel(q_ref, f_ref, i_ref, og_ref, lbl_ref, ng_ref, o_ref, st_ref, *, lt, group):
    c_len, sub = HGRN_CHUNK, HGRN_SUB
    n_sub = c_len // sub
    dk = B_KEY_DIM
    heads = [slice(g * LANES, (g + 1) * LANES) for g in range(group)]

    @pl.when(pl.program_id(2) == 0)
    def _():
        st_ref[...] = jnp.zeros(st_ref.shape, F32)

    lg = lbl_ref[...]
    e = jnp.exp(lg - jnp.max(lg, axis=0, keepdims=True))
    lb = e[0:1, :] / jnp.sum(e, axis=0, keepdims=True)
    ng = ng_ref[...]

    tri = (lax.broadcasted_iota(jnp.int32, (c_len, c_len), 0)
           >= lax.broadcasted_iota(jnp.int32, (c_len, c_len), 1)).astype(BF16)
    width = group * dk
    half = sub // 2
    n_half = c_len // half
    rowi = lax.broadcasted_iota(jnp.int32, (n_half, half, width), 1)
    same_sub = (lax.broadcasted_iota(jnp.int32, (n_sub * half, n_sub * half), 0) // half
                == lax.broadcasted_iota(jnp.int32, (n_sub * half, n_sub * half), 1) // half)

    def chunk(c, carry):
        rows = pl.ds(pl.multiple_of(c * c_len, c_len), c_len)
        qb = q_ref[rows, :].astype(F32)
        fb = f_ref[rows, :].astype(F32)
        v = i_ref[rows, :].astype(F32)
        og = og_ref[rows, :].astype(F32)

        q = qb * _sigmoid(qb) * (dk ** -0.5)
        fg = lb + (1.0 - lb) * _sigmoid(fb)
        logf = jnp.log2(fg)
        k = 1.0 - fg

        hi = logf.astype(BF16)
        r1 = logf - hi.astype(F32)
        mid = r1.astype(BF16)
        lo = (r1 - mid.astype(F32)).astype(BF16)
        b = _dot(tri, hi) + _dot(tri, mid) + _dot(tri, lo)

        states = [st_ref[g] for g in range(group)]
        qe = (q * jnp.exp2(b)).astype(BF16)
        o_inter = [_dot_nt(qe[:, hs], states[g].astype(BF16)) for g, hs in enumerate(heads)]

        v16 = v.astype(BF16)
        pair_a = []
        for s_i in range(1, n_sub):
            lo_r = s_i * sub
            ref_b = b[lo_r - 1:lo_r, :]
            qt = (q[lo_r:lo_r + sub, :] * jnp.exp2(b[lo_r:lo_r + sub, :] - ref_b)).astype(BF16)
            kt = (k[:lo_r, :] * jnp.exp2(ref_b - b[:lo_r, :])).astype(BF16)
            pair_a.append([_dot_nt(qt[:, hs], kt[:, hs]).astype(BF16) for hs in heads])
        pair_o = [[_dot(pair_a[s_i - 1][g], v16[:s_i * sub, hs]) for s_i in range(1, n_sub)]
                  for g, hs in enumerate(heads)]

        b3 = b.reshape(n_sub, sub, width)
        ref_h = b3[:, half - 1:half, :]
        q_h = q.reshape(n_sub, sub, width)[:, half:, :] * jnp.exp2(b3[:, half:, :] - ref_h)
        k_h = k.reshape(n_sub, sub, width)[:, :half, :] * jnp.exp2(ref_h - b3[:, :half, :])
        q_h = q_h.reshape(n_sub * half, width).astype(BF16)
        k_h = k_h.reshape(n_sub * half, width).astype(BF16)
        v_h = v.reshape(n_sub, sub, width)[:, :half, :].reshape(n_sub * half, width).astype(BF16)
        half_a = [jnp.where(same_sub, _dot_nt(q_h[:, hs], k_h[:, hs]), 0.0).astype(BF16) for hs in heads]
        half_o = [_dot(half_a[g], v_h[:, hs]) for g, hs in enumerate(heads)]

        b_last = b[c_len - 1:c_len, :]
        kl = (k * jnp.exp2(b_last - b)).astype(BF16)
        decay = jnp.exp2(b_last)
        for g, hs in enumerate(heads):
            st_ref[g] = states[g] * decay[:, hs] + _dot(v[:, hs].T.astype(BF16), kl[:, hs])

        b8 = b.reshape(n_half, half, width)
        q8 = q.reshape(n_half, half, width)
        k8 = k.reshape(n_half, half, width)
        v8 = v.reshape(n_half, half, width)
        od = [jnp.zeros((n_half, half, dk), F32) for _ in heads]
        for s_o in range(half):
            dec = jnp.exp2(jnp.where(rowi >= s_o, b8 - b8[:, s_o:s_o + 1, :], NEG))
            gq = q8 * dec * k8[:, s_o:s_o + 1, :]
            for g, hs in enumerate(heads):
                od[g] = od[g] + jnp.sum(gq[:, :, hs], axis=-1, keepdims=True) * v8[:, s_o:s_o + 1, hs]

        gate = og * _sigmoid(og)
        for g, hs in enumerate(heads):
            o = o_inter[g] + jnp.concatenate([jnp.zeros((sub, dk), F32)] + pair_o[g], axis=0)
            second = jnp.concatenate([jnp.zeros((n_sub, half, dk), F32), half_o[g].reshape(n_sub, half, dk)], axis=1)
            o = o + second.reshape(c_len, dk) + od[g].reshape(c_len, dk)
            o = o * lax.rsqrt(jnp.mean(o * o, axis=-1, keepdims=True) + EPS) * ng[:, hs]
            o_ref[rows, hs] = (o * gate[:, hs]).astype(BF16)
        return carry

    lax.fori_loop(0, lt // c_len, chunk, 0)


def _hgrn(z, lb_logits, norm_g, batch, seq, lt, group):
    nt = seq // lt
    width = group * LANES
    d = B_HEADS * LANES
    seg = lambda off: pl.BlockSpec((lt, width), lambda b, hh, t: (b * nt + t, off * (d // width) + hh))
    return pl.pallas_call(
        functools.partial(_hgrn_kernel, lt=lt, group=group),
        grid=(batch, B_HEADS // group, nt),
        in_specs=[
            seg(1), seg(2), seg(3), seg(4),
            pl.BlockSpec((lb_logits.shape[0], width), lambda b, hh, t: (0, hh)),
            pl.BlockSpec((1, width), lambda b, hh, t: (0, hh)),
        ],
        out_specs=pl.BlockSpec((lt, width), lambda b, hh, t: (b * nt + t, hh)),
        out_shape=jax.ShapeDtypeStruct((batch * seq, d), BF16),
        scratch_shapes=[pltpu.VMEM((group, LANES, B_KEY_DIM), F32)],
        compiler_params=_params(("parallel", "parallel", "arbitrary")),
        name="hgrn",
    )(z, z, z, z, lb_logits, norm_g)


def _merge_kernel(x_ref, a_ref, hg_ref, ga_ref, gb_ref, wa_ref, wb_ref, wo_ref, gx_ref, wq_ref,
                  km_ref, vm_ref, wox_ref, o_ref):
    ya = _dot(a_ref[...], wa_ref[...])
    yb = _dot(hg_ref[...], wb_ref[...])
    mg = _sigmoid(ga_ref[...].astype(F32)) * ya + _sigmoid(gb_ref[...].astype(F32)) * yb
    h = x_ref[...] + _dot(mg.astype(BF16), wo_ref[...])

    hn = _rms(h, gx_ref[...]).astype(BF16)
    q = _dot(hn, wq_ref[...])
    d = q.shape[1]
    xd = d // X_HEADS
    outs = []
    for hh in range(X_HEADS):
        sl = slice(hh * xd, (hh + 1) * xd)
        s = _dot_nt(q[:, sl].astype(BF16), km_ref[:, sl]) * (xd ** -0.5)
        p = jnp.exp(s - jnp.max(s, axis=-1, keepdims=True))
        oh = _dot(p.astype(BF16), vm_ref[:, sl])
        outs.append(oh / jnp.sum(p, axis=-1, keepdims=True))
    o = jnp.concatenate(outs, axis=-1)
    o_ref[...] = h + _dot(o.astype(BF16), wox_ref[...])


def _merge(x2d, attn, hg, z, wa, wb, wo, gx, wq, kvm, wox, batch, seq, n_mem, tm):
    d = x2d.shape[1]
    nt = seq // tm
    gcol = (z.shape[1] - 2 * d) // d
    row = lambda b, t: (b * nt + t, 0)
    full = lambda b, t: (0, 0)
    wspec = pl.BlockSpec((d, d), full)
    return pl.pallas_call(
        _merge_kernel,
        grid=(batch, nt),
        in_specs=[
            pl.BlockSpec((tm, d), row),
            pl.BlockSpec((tm, d), row),
            pl.BlockSpec((tm, d), row),
            pl.BlockSpec((tm, d), lambda b, t: (b * nt + t, gcol)),
            pl.BlockSpec((tm, d), lambda b, t: (b * nt + t, gcol + 1)),
            wspec, wspec, wspec,
            pl.BlockSpec((1, d), full),
            wspec,
            pl.BlockSpec((n_mem, d), lambda b, t: (b, 0)),
            pl.BlockSpec((n_mem, d), lambda b, t: (b, 1)),
            wspec,
        ],
        out_specs=pl.BlockSpec((tm, d), row),
        out_shape=jax.ShapeDtypeStruct(x2d.shape, F32),
        compiler_params=_params(("parallel", "parallel")),
        name="merge",
    )(x2d, attn, hg, z, z, wa, wb, wo, gx, wq, kvm, kvm, wox)


def _ffn_kernel(h_ref, g_ref, wi_ref, wo_ref, gf_ref, o_ref, *, hidden, bounds):
    h = h_ref[...]
    hn = _rms(h, g_ref[...]).astype(BF16)
    acc = h
    for lo, hi in bounds:
        gate = _dot(hn, wi_ref[:, lo:hi])
        up = _dot(hn, wi_ref[:, hidden + lo:hidden + hi])
        act = (gate * _sigmoid(gate) * up).astype(BF16)
        acc = acc + _dot(act, wo_ref[lo:hi, :])
    o_ref[...] = _rms(acc, gf_ref[...])


def _ffn(h2d, g, wi, wo, gf, tm):
    t, d = h2d.shape
    hidden = wo.shape[0]
    step = 1024
    bounds = tuple((lo, min(lo + step, hidden)) for lo in range(0, hidden, step))
    full = lambda i: (0, 0)
    return pl.pallas_call(
        functools.partial(_ffn_kernel, hidden=hidden, bounds=bounds),
        grid=(t // tm,),
        in_specs=[
            pl.BlockSpec((tm, d), lambda i: (i, 0)),
            pl.BlockSpec((1, d), full),
            pl.BlockSpec(wi.shape, full, pipeline_mode=pl.Buffered(1)),
            pl.BlockSpec(wo.shape, full, pipeline_mode=pl.Buffered(1)),
            pl.BlockSpec((1, d), full),
        ],
        out_specs=pl.BlockSpec((tm, d), lambda i: (i, 0)),
        out_shape=jax.ShapeDtypeStruct(h2d.shape, F32),
        compiler_params=_params(("parallel",)),
        name="ffn",
    )(h2d, g, wi, wo, gf)


def _rope_tables(positions):
    half = A_HEAD_DIM // 2
    inv_freq = jnp.power(ROPE_THETA, -jnp.arange(0, A_HEAD_DIM, 2, dtype=F32) / A_HEAD_DIM)
    pos = positions.astype(F32).reshape(-1)
    lane = jnp.arange(LANES)
    ang = pos[:, None] * inv_freq[lane % half][None, :]
    sign = jnp.where((lane % A_HEAD_DIM) < half, -1.0, 1.0).astype(F32)
    ang_t = inv_freq[:, None] * pos[None, :]
    return jnp.cos(ang), jnp.sin(ang) * sign, jnp.cos(ang_t), jnp.sin(ang_t)


def kernel(x, mem, positions, norm_mix_g, w_in, hgrn_lb_logits, hgrn_norm_g, w_br_attn, w_br_hgrn, w_out,
           norm_x_g, norm_mem_g, wq_x, wkv_x, wo_x, norm_ffn_g, w_ffn_in, w_ffn_out, final_norm_g):
    batch, seq, d = x.shape
    n_mem = mem.shape[1]
    depth = w_in.shape[0]
    assert depth == 1 and d == A_HEADS * A_HEAD_DIM == B_HEADS * LANES and B_KEY_DIM == LANES
    tm = min(512, seq)
    tq = MOBA_QUERY_BLOCKS * MOBA_BLOCK
    assert seq % tq == 0 and seq % tm == 0 and w_in.shape[2] == 9 * d
    lt = min(1024, seq)
    x2d = x.reshape(batch * seq, d)
    cos_t, sin_t, cos_tt, sin_tt = _rope_tables(positions)
    row = lambda g: g.reshape(1, -1).astype(F32)
    bf = lambda w: w.astype(BF16)

    w_in16 = bf(w_in[0])
    wt_qv = jnp.stack([w_in16[:, :d].T, w_in16[:, 2 * d:3 * d].T])
    w_rest = jnp.concatenate([w_in16[:, d:2 * d], w_in16[:, 3 * d:]], axis=1)

    kvm = _memkv(mem.reshape(batch * n_mem, d), row(norm_mem_g[0]), bf(wkv_x[0]), n_mem)
    z, qt, vt = _inproj(x2d, row(norm_mix_g[0]), w_rest, wt_qv, cos_t, sin_t, cos_tt, sin_tt, tm)
    attn = _moba(z, qt, vt, batch, seq, tq)
    hg = _hgrn(z, hgrn_lb_logits.astype(F32), row(hgrn_norm_g[0]), batch, seq, lt, HGRN_GROUP)
    h = _merge(x2d, attn, hg, z, bf(w_br_attn[0]), bf(w_br_hgrn[0]), bf(w_out[0]), row(norm_x_g[0]),
               bf(wq_x[0]), kvm, bf(wo_x[0]), batch, seq, n_mem, tm)
    out = _ffn(h, row(norm_ffn_g[0]), bf(w_ffn_in[0]), bf(w_ffn_out[0]), row(final_norm_g), tm)
    return out.reshape(batch, seq, d)
```

```python
import functools

import jax
import jax.numpy as jnp
from jax import lax
from jax.experimental import pallas as pl
from jax.experimental.pallas import tpu as pltpu

F32 = jnp.float32
BF16 = jnp.bfloat16

EPS = 1e-6
NEG = -1e30
ROPE_THETA = 10000.0

A_HEADS = 16
A_HEAD_DIM = 64
MOBA_BLOCK = 256
MOBA_TOPK = 3
MOBA_QUERY_BLOCKS = 4
MOBA_LOOKAHEAD = 3
B_HEADS = 8
B_KEY_DIM = 128
X_HEADS = 4
HGRN_CHUNK = 64
HGRN_SUB = 16
HGRN_GROUP = 4

LANES = 128
SUM_ROWS = 16
LOG2_E = 1.4426950408889634
VMEM_LIMIT = 56 * 1024 * 1024


def _params(sem, flags=None):
    return pltpu.CompilerParams(dimension_semantics=sem, vmem_limit_bytes=VMEM_LIMIT, flags=flags)


def _rms(x, g):
    return x * lax.rsqrt(jnp.mean(x * x, axis=-1, keepdims=True) + EPS) * g


def _dot(a, b):
    return jnp.dot(a, b, preferred_element_type=F32)


def _dot_nt(a, b):
    return lax.dot_general(a, b, (((1,), (1,)), ((), ())), preferred_element_type=F32)


def _sigmoid(x):
    return 1.0 / (1.0 + jnp.exp(-x))


def _split_heads(q_t):
    row = lax.broadcasted_iota(jnp.int32, q_t.shape, 0)
    zero = jnp.zeros_like(q_t)
    return [jnp.where(row < A_HEAD_DIM, q_t, zero), jnp.where(row >= A_HEAD_DIM, q_t, zero)]


def _memkv_kernel(mem_ref, g_ref, w_ref, o_ref):
    y = _rms(mem_ref[...], g_ref[...])
    o_ref[...] = _dot(y.astype(BF16), w_ref[...]).astype(BF16)


def _memkv(mem2d, g, w, n_mem):
    rows, d = mem2d.shape
    return pl.pallas_call(
        _memkv_kernel,
        grid=(rows // n_mem,),
        in_specs=[
            pl.BlockSpec((n_mem, d), lambda b: (b, 0)),
            pl.BlockSpec((1, d), lambda b: (0, 0)),
            pl.BlockSpec(w.shape, lambda b: (0, 0)),
        ],
        out_specs=pl.BlockSpec((n_mem, w.shape[1]), lambda b: (b, 0)),
        out_shape=jax.ShapeDtypeStruct((rows, w.shape[1]), BF16),
        compiler_params=_params(("parallel",)),
        name="memkv",
    )(mem2d, g, w)


def _inproj_kernel(x_ref, g_ref, w_ref, wt_ref, cos_ref, sin_ref, cost_ref, sint_ref, zero_ref,
                   z_ref, qt_ref, vt_ref, sel_ref, km_ref, *, tm, seq):
    hd, half = A_HEAD_DIM, A_HEAD_DIM // 2
    d = x_ref.shape[1]
    nb = seq // MOBA_BLOCK

    @pl.when(pl.program_id(0) == 0)
    def _():
        km_ref[...] = jnp.zeros(km_ref.shape, F32)

    xn = _rms(x_ref[...], g_ref[...]).astype(BF16)

    acc = _dot_nt(wt_ref[0], xn)
    cos = cost_ref[...]
    sin = sint_ref[...]
    scale = hd ** -0.5 * LOG2_E
    for h in range(d // hd):
        t1 = acc[h * hd:h * hd + half, :]
        t2 = acc[h * hd + half:(h + 1) * hd, :]
        qt_ref[h * hd:h * hd + half, :] = ((t1 * cos - t2 * sin) * scale).astype(BF16)
        qt_ref[h * hd + half:(h + 1) * hd, :] = ((t2 * cos + t1 * sin) * scale).astype(BF16)

    acc = _dot(xn, w_ref[:, :d])
    cos = cos_ref[...]
    sin = sin_ref[...]
    lane = lax.broadcasted_iota(jnp.int32, cos.shape, 1)
    low = (lane % hd) < half
    for c in range(d // LANES):
        t = acc[:, c * LANES:(c + 1) * LANES]
        rot = jnp.where(low, pltpu.roll(t, LANES - half, 1), pltpu.roll(t, half, 1))
        z_ref[:, c * LANES:(c + 1) * LANES] = (t * cos + rot * sin).astype(BF16)

    blk = MOBA_BLOCK
    tile = pl.program_id(0) % (seq // tm)
    for n in range(tm // blk):
        km_ref[pl.ds(tile * (tm // blk) + n, 1), :] = jnp.mean(
            z_ref[n * blk:(n + 1) * blk, :d].astype(F32), axis=0, keepdims=True)
    km = km_ref[...]
    km_hi = km.astype(BF16)
    km_lo = (km - km_hi.astype(F32)).astype(BF16)
    blk_id = lax.broadcasted_iota(jnp.int32, (nb, tm), 0)
    q_blk = tile * (tm // blk) + lax.broadcasted_iota(jnp.int32, (nb, tm), 1) // blk
    gates = []
    for p in range(d // LANES):
        ps = slice(p * LANES, (p + 1) * LANES)
        gates += [_dot(km_hi[:, ps], qz) + _dot(km_lo[:, ps], qz) for qz in _split_heads(qt_ref[ps, :])]

    def select(head, after):
        order = jnp.where(after[tm - 1:tm, :tm] > 0.0, zero_ref[...], -zero_ref[...])
        g = jnp.where(blk_id < q_blk + order, gates[head], -jnp.inf)
        sel = jnp.zeros((nb, tm), F32)
        for _ in range(MOBA_TOPK):
            mx = jnp.max(g, axis=0, keepdims=True)
            first = jnp.min(jnp.where(g == mx, blk_id, nb), axis=0, keepdims=True)
            pick = jnp.logical_and(blk_id == first, mx > -jnp.inf)
            sel = jnp.where(pick, 1.0, sel)
            g = jnp.where(blk_id == first, -jnp.inf, g)
        sel_ref[head * nb:(head + 1) * nb, :] = sel

    n_rest = w_ref.shape[1] // d
    per_slot = -(-len(gates) // n_rest)
    todo = list(range(len(gates)))

    acc = _dot_nt(wt_ref[1], xn)
    for n in range(tm // MOBA_BLOCK):
        vt_ref[n] = acc[:, n * MOBA_BLOCK:(n + 1) * MOBA_BLOCK].astype(BF16)
    for head in todo[:per_slot]:
        select(head, acc)

    for s in range(1, n_rest):
        acc = _dot(xn, w_ref[:, s * d:(s + 1) * d])
        z_ref[:, s * d:(s + 1) * d] = acc.astype(BF16)
        for head in todo[s * per_slot:(s + 1) * per_slot]:
            select(head, acc)


def _inproj(x2d, g, w, wt, cos_t, sin_t, cos_tt, sin_tt, tm, seq):
    t, d = x2d.shape
    blk = MOBA_BLOCK
    nb = seq // blk
    full = lambda i: (0, 0)
    return pl.pallas_call(
        functools.partial(_inproj_kernel, tm=tm, seq=seq),
        grid=(t // tm,),
        in_specs=[
            pl.BlockSpec((tm, d), lambda i: (i, 0)),
            pl.BlockSpec((1, d), full),
            pl.BlockSpec(w.shape, full, pipeline_mode=pl.Buffered(1)),
            pl.BlockSpec(wt.shape, lambda i: (0, 0, 0), pipeline_mode=pl.Buffered(1)),
            pl.BlockSpec((tm, LANES), lambda i: (i, 0)),
            pl.BlockSpec((tm, LANES), lambda i: (i, 0)),
            pl.BlockSpec((A_HEAD_DIM // 2, tm), lambda i: (0, i)),
            pl.BlockSpec((A_HEAD_DIM // 2, tm), lambda i: (0, i)),
            pl.BlockSpec((1, tm), full),
        ],
        out_specs=[
            pl.BlockSpec((tm, w.shape[1]), lambda i: (i, 0)),
            pl.BlockSpec((d, tm), lambda i: (0, i)),
            pl.BlockSpec((tm // blk, d, blk), lambda i: (i, 0, 0)),
            pl.BlockSpec((A_HEADS * nb, tm), lambda i: (0, i)),
        ],
        out_shape=[
            jax.ShapeDtypeStruct((t, w.shape[1]), BF16),
            jax.ShapeDtypeStruct((d, t), BF16),
            jax.ShapeDtypeStruct((t // blk, d, blk), BF16),
            jax.ShapeDtypeStruct((A_HEADS * nb, t), F32),
        ],
        scratch_shapes=[pltpu.VMEM((nb, d), F32)],
        compiler_params=_params(("arbitrary",)),
        name="inproj",
    )(x2d, g, w, wt, cos_t, sin_t, cos_tt, sin_tt, jnp.zeros((1, tm), jnp.int32))


def _moba_kernel(q_ref, k_ref, v_ref, sel_ref, o_ref, m_ref, acc_ref, s_ref, *, nb, blk, tq):
    i = pl.program_id(2)
    hd = A_HEAD_DIM
    halves = tq // blk

    q_t = q_ref[...]
    qz = _split_heads(q_t)

    m_ref[...] = jnp.full(m_ref.shape, NEG, F32)
    acc_ref[...] = jnp.zeros(acc_ref.shape, F32)
    ones = jnp.ones((SUM_ROWS, blk), BF16)

    causal = (lax.broadcasted_iota(jnp.int32, (blk, blk), 0)
              <= lax.broadcasted_iota(jnp.int32, (blk, blk), 1))

    groups = [(h, c) for h in range(2) for c in range(halves)]

    def score(slot, j, g):
        def run():
            h, c = groups[g]
            kj = k_ref[pl.ds(pl.multiple_of(j * blk, blk), blk), :]
            s_ref[slot, g] = _dot(kj, qz[h][:, c * blk:(c + 1) * blk])
        return run

    def finish(slot, j, g, own):
        def run():
            h, c = groups[g]
            cs = slice(c * blk, (c + 1) * blk)
            s = s_ref[slot, g]
            m_old = m_ref[h:h + 1, cs]
            if own:
                s = jnp.where(causal, s, NEG)
                m_new = jnp.maximum(m_old, jnp.max(s, axis=0, keepdims=True))
                m_use = m_new
            else:
                on = sel_ref[pl.ds(h * nb + j, 1), cs] > 0.0
                m_new = jnp.where(on, jnp.maximum(m_old, jnp.max(s, axis=0, keepdims=True)), m_old)
                m_use = jnp.where(on, m_new, -NEG)
            p = jnp.exp2(s - m_use).astype(BF16)
            pv = _dot(jnp.concatenate([v_ref[j, h * hd:(h + 1) * hd, :], ones], axis=0), p)
            acc_ref[h, :, cs] = jnp.exp2(m_old - m_new) * acc_ref[h, :, cs] + pv
            m_ref[h:h + 1, cs] = m_new
        return run

    def overlap(scores, finishes):
        for run in scores[:MOBA_LOOKAHEAD]:
            run()
        rest = scores[MOBA_LOOKAHEAD:]
        for run in finishes:
            run()
            if rest:
                rest.pop(0)()
        for run in rest:
            run()

    everyone = range(len(groups))
    overlap([score(0, 0, g) for g in everyone], [])

    def body(jj, carry):
        j0 = 2 * jj
        overlap([score(1, j0 + 1, g) for g in everyone], [finish(0, j0, g, False) for g in everyone])
        overlap([score(0, j0 + 2, g) for g in everyone], [finish(1, j0 + 1, g, False) for g in everyone])
        return carry

    lax.fori_loop(0, i * (halves // 2), body, 0)

    base = i * halves
    for t in range(halves):
        later = [g for g in everyone if groups[g][1] > t]
        overlap([score((t + 1) % 2, base + t + 1, g) for g in later],
                [finish(t % 2, base + t, g, groups[g][1] == t) for g in everyone if groups[g][1] >= t])

    out_t = jnp.concatenate([acc_ref[h, :hd, :] / acc_ref[h, hd:hd + 1, :] for h in range(2)], axis=0)
    o_ref[...] = out_t.T.astype(BF16)


def _moba(z, qt, vt, sel, batch, seq, tq):
    blk = MOBA_BLOCK
    nb = seq // blk
    nq = seq // tq
    d = qt.shape[0]
    pairs = d // LANES
    return pl.pallas_call(
        functools.partial(_moba_kernel, nb=nb, blk=blk, tq=tq),
        grid=(batch, pairs, nq),
        in_specs=[
            pl.BlockSpec((LANES, tq), lambda b, p, i: (p, b * nq + i)),
            pl.BlockSpec((seq, LANES), lambda b, p, i: (b, p)),
            pl.BlockSpec((nb, LANES, blk), lambda b, p, i: (b, p, 0)),
            pl.BlockSpec((2 * nb, tq), lambda b, p, i: (p, b * nq + i)),
        ],
        out_specs=pl.BlockSpec((tq, LANES), lambda b, p, i: (b * nq + i, p)),
        out_shape=jax.ShapeDtypeStruct((batch * seq, d), BF16),
        scratch_shapes=[
            pltpu.VMEM((2, tq), F32),
            pltpu.VMEM((2, A_HEAD_DIM + SUM_ROWS, tq), F32),
            pltpu.VMEM((2, 2 * (tq // blk), blk, blk), F32),
        ],
        compiler_params=_params(("parallel", "parallel", "arbitrary")),
        name="moba",
    )(qt, z, vt, sel)


def _hgrn_kernel(q_ref, f_ref, i_ref, og_ref, lbl_ref, ng_ref, o_ref, st_ref, *, lt, group):
    c_len, sub = HGRN_CHUNK, HGRN_SUB
    n_sub = c_len // sub
    dk = B_KEY_DIM
    heads = [slice(g * LANES, (g + 1) * LANES) for g in range(group)]

    @pl.when(pl.program_id(2) == 0)
    def _():
        st_ref[...] = jnp.zeros(st_ref.shape, F32)

    lg = lbl_ref[...]
    e = jnp.exp(lg - jnp.max(lg, axis=0, keepdims=True))
    lb = e[0:1, :] / jnp.sum(e, axis=0, keepdims=True)
    ng = ng_ref[...]

    tri = (lax.broadcasted_iota(jnp.int32, (c_len, c_len), 0)
           >= lax.broadcasted_iota(jnp.int32, (c_len, c_len), 1)).astype(BF16)
    width = group * dk
    half = sub // 2
    n_half = c_len // half
    rowi = lax.broadcasted_iota(jnp.int32, (n_half, half, width), 1)
    same_sub = (lax.broadcasted_iota(jnp.int32, (n_sub * half, n_sub * half), 0) // half
                == lax.broadcasted_iota(jnp.int32, (n_sub * half, n_sub * half), 1) // half)

    def chunk(c, carry):
        rows = pl.ds(pl.multiple_of(c * c_len, c_len), c_len)
        qb = q_ref[rows, :].astype(F32)
        fb = f_ref[rows, :].astype(F32)
        v = i_ref[rows, :].astype(F32)
        og = og_ref[rows, :].astype(F32)

        q = qb * _sigmoid(qb) * (dk ** -0.5)
        fg = lb + (1.0 - lb) * _sigmoid(fb)
        logf = jnp.log2(fg)
        k = 1.0 - fg

        hi = logf.astype(BF16)
        r1 = logf - hi.astype(F32)
        mid = r1.astype(BF16)
        lo = (r1 - mid.astype(F32)).astype(BF16)
        b = _dot(tri, hi) + _dot(tri, mid) + _dot(tri, lo)

        states = [st_ref[g] for g in range(group)]
        qe = (q * jnp.exp2(b)).astype(BF16)
        o_inter = [_dot_nt(qe[:, hs], states[g].astype(BF16)) for g, hs in enumerate(heads)]

        v16 = v.astype(BF16)
        pair_a = []
        for s_i in range(1, n_sub):
            lo_r = s_i * sub
            ref_b = b[lo_r - 1:lo_r, :]
            qt = (q[lo_r:lo_r + sub, :] * jnp.exp2(b[lo_r:lo_r + sub, :] - ref_b)).astype(BF16)
            kt = (k[:lo_r, :] * jnp.exp2(ref_b - b[:lo_r, :])).astype(BF16)
            pair_a.append([_dot_nt(qt[:, hs], kt[:, hs]).astype(BF16) for hs in heads])
        pair_o = [[_dot(pair_a[s_i - 1][g], v16[:s_i * sub, hs]) for s_i in range(1, n_sub)]
                  for g, hs in enumerate(heads)]

        b3 = b.reshape(n_sub, sub, width)
        ref_h = b3[:, half - 1:half, :]
        q_h = q.reshape(n_sub, sub, width)[:, half:, :] * jnp.exp2(b3[:, half:, :] - ref_h)
        k_h = k.reshape(n_sub, sub, width)[:, :half, :] * jnp.exp2(ref_h - b3[:, :half, :])
        q_h = q_h.reshape(n_sub * half, width).astype(BF16)
        k_h = k_h.reshape(n_sub * half, width).astype(BF16)
        v_h = v.reshape(n_sub, sub, width)[:, :half, :].reshape(n_sub * half, width).astype(BF16)
        half_a = [jnp.where(same_sub, _dot_nt(q_h[:, hs], k_h[:, hs]), 0.0).astype(BF16) for hs in heads]
        half_o = [_dot(half_a[g], v_h[:, hs]) for g, hs in enumerate(heads)]

        b_last = b[c_len - 1:c_len, :]
        kl = (k * jnp.exp2(b_last - b)).astype(BF16)
        decay = jnp.exp2(b_last)
        for g, hs in enumerate(heads):
            st_ref[g] = states[g] * decay[:, hs] + _dot(v[:, hs].T.astype(BF16), kl[:, hs])

        b8 = b.reshape(n_half, half, width)
        q8 = q.reshape(n_half, half, width)
        k8 = k.reshape(n_half, half, width)
        v8 = v.reshape(n_half, half, width)
        od = [jnp.zeros((n_half, half, dk), F32) for _ in heads]
        for s_o in range(half):
            dec = jnp.exp2(jnp.where(rowi >= s_o, b8 - b8[:, s_o:s_o + 1, :], NEG))
            gq = q8 * dec * k8[:, s_o:s_o + 1, :]
            for g, hs in enumerate(heads):
                od[g] = od[g] + jnp.sum(gq[:, :, hs], axis=-1, keepdims=True) * v8[:, s_o:s_o + 1, hs]

        gate = og * _sigmoid(og)
        for g, hs in enumerate(heads):
            o = o_inter[g] + jnp.concatenate([jnp.zeros((sub, dk), F32)] + pair_o[g], axis=0)
            second = jnp.concatenate([jnp.zeros((n_sub, half, dk), F32), half_o[g].reshape(n_sub, half, dk)], axis=1)
            o = o + second.reshape(c_len, dk) + od[g].reshape(c_len, dk)
            o = o * lax.rsqrt(jnp.mean(o * o, axis=-1, keepdims=True) + EPS) * ng[:, hs]
            o_ref[rows, hs] = (o * gate[:, hs]).astype(BF16)
        return carry

    lax.fori_loop(0, lt // c_len, chunk, 0)


def _hgrn(z, lb_logits, norm_g, batch, seq, lt, group):
    nt = seq // lt
    width = group * LANES
    d = B_HEADS * LANES
    seg = lambda off: pl.BlockSpec((lt, width), lambda b, hh, t: (b * nt + t, off * (d // width) + hh))
    return pl.pallas_call(
        functools.partial(_hgrn_kernel, lt=lt, group=group),
        grid=(batch, B_HEADS // group, nt),
        in_specs=[
            seg(1), seg(2), seg(3), seg(4),
            pl.BlockSpec((lb_logits.shape[0], width), lambda b, hh, t: (0, hh)),
            pl.BlockSpec((1, width), lambda b, hh, t: (0, hh)),
        ],
        out_specs=pl.BlockSpec((lt, width), lambda b, hh, t: (b * nt + t, hh)),
        out_shape=jax.ShapeDtypeStruct((batch * seq, d), BF16),
        scratch_shapes=[pltpu.VMEM((group, LANES, B_KEY_DIM), F32)],
        compiler_params=_params(("parallel", "parallel", "arbitrary")),
        name="hgrn",
    )(z, z, z, z, lb_logits, norm_g)


def _merge_kernel(x_ref, a_ref, hg_ref, ga_ref, gb_ref, wa_ref, wb_ref, wo_ref, gx_ref, wq_ref,
                  km_ref, vm_ref, wox_ref, o_ref):
    ya = _dot(a_ref[...], wa_ref[...])
    yb = _dot(hg_ref[...], wb_ref[...])
    mg = _sigmoid(ga_ref[...].astype(F32)) * ya + _sigmoid(gb_ref[...].astype(F32)) * yb
    h = x_ref[...] + _dot(mg.astype(BF16), wo_ref[...])

    hn = _rms(h, gx_ref[...]).astype(BF16)
    q = _dot(hn, wq_ref[...])
    d = q.shape[1]
    xd = d // X_HEADS
    outs = []
    for hh in range(X_HEADS):
        sl = slice(hh * xd, (hh + 1) * xd)
        s = _dot_nt(q[:, sl].astype(BF16), km_ref[:, sl]) * (xd ** -0.5)
        p = jnp.exp(s - jnp.max(s, axis=-1, keepdims=True))
        oh = _dot(p.astype(BF16), vm_ref[:, sl])
        outs.append(oh / jnp.sum(p, axis=-1, keepdims=True))
    o = jnp.concatenate(outs, axis=-1)
    o_ref[...] = h + _dot(o.astype(BF16), wox_ref[...])


def _merge(x2d, attn, hg, z, wa, wb, wo, gx, wq, kvm, wox, batch, seq, n_mem, tm):
    d = x2d.shape[1]
    nt = seq // tm
    gcol = (z.shape[1] - 2 * d) // d
    row = lambda b, t: (b * nt + t, 0)
    full = lambda b, t: (0, 0)
    wspec = pl.BlockSpec((d, d), full)
    return pl.pallas_call(
        _merge_kernel,
        grid=(batch, nt),
        in_specs=[
            pl.BlockSpec((tm, d), row),
            pl.BlockSpec((tm, d), row),
            pl.BlockSpec((tm, d), row),
            pl.BlockSpec((tm, d), lambda b, t: (b * nt + t, gcol)),
            pl.BlockSpec((tm, d), lambda b, t: (b * nt + t, gcol + 1)),
            wspec, wspec, wspec,
            pl.BlockSpec((1, d), full),
            wspec,
            pl.BlockSpec((n_mem, d), lambda b, t: (b, 0)),
            pl.BlockSpec((n_mem, d), lambda b, t: (b, 1)),
            wspec,
        ],
        out_specs=pl.BlockSpec((tm, d), row),
        out_shape=jax.ShapeDtypeStruct(x2d.shape, F32),
        compiler_params=_params(("parallel", "parallel")),
        name="merge",
    )(x2d, attn, hg, z, z, wa, wb, wo, gx, wq, kvm, kvm, wox)


def _ffn_kernel(h_ref, g_ref, wi_ref, wo_ref, gf_ref, o_ref, *, hidden, bounds):
    h = h_ref[...]
    hn = _rms(h, g_ref[...]).astype(BF16)
    acc = h
    for lo, hi in bounds:
        gate = _dot(hn, wi_ref[:, lo:hi])
        up = _dot(hn, wi_ref[:, hidden + lo:hidden + hi])
        act = (gate * _sigmoid(gate) * up).astype(BF16)
        acc = acc + _dot(act, wo_ref[lo:hi, :])
    o_ref[...] = _rms(acc, gf_ref[...])


def _ffn(h2d, g, wi, wo, gf, tm):
    t, d = h2d.shape
    hidden = wo.shape[0]
    step = 1024
    bounds = tuple((lo, min(lo + step, hidden)) for lo in range(0, hidden, step))
    full = lambda i: (0, 0)
    return pl.pallas_call(
        functools.partial(_ffn_kernel, hidden=hidden, bounds=bounds),
        grid=(t // tm,),
        in_specs=[
            pl.BlockSpec((tm, d), lambda i: (i, 0)),
            pl.BlockSpec((1, d), full),
            pl.BlockSpec(wi.shape, full, pipeline_mode=pl.Buffered(1)),
            pl.BlockSpec(wo.shape, full, pipeline_mode=pl.Buffered(1)),
            pl.BlockSpec((1, d), full),
        ],
        out_specs=pl.BlockSpec((tm, d), lambda i: (i, 0)),
        out_shape=jax.ShapeDtypeStruct(h2d.shape, F32),
        compiler_params=_params(("parallel",)),
        name="ffn",
    )(h2d, g, wi, wo, gf)


def _rope_tables(positions):
    half = A_HEAD_DIM // 2
    inv_freq = jnp.power(ROPE_THETA, -jnp.arange(0, A_HEAD_DIM, 2, dtype=F32) / A_HEAD_DIM)
    pos = positions.astype(F32).reshape(-1)
    lane = jnp.arange(LANES)
    ang = pos[:, None] * inv_freq[lane % half][None, :]
    sign = jnp.where((lane % A_HEAD_DIM) < half, -1.0, 1.0).astype(F32)
    ang_t = inv_freq[:, None] * pos[None, :]
    return jnp.cos(ang), jnp.sin(ang) * sign, jnp.cos(ang_t), jnp.sin(ang_t)


def kernel(x, mem, positions, norm_mix_g, w_in, hgrn_lb_logits, hgrn_norm_g, w_br_attn, w_br_hgrn, w_out,
           norm_x_g, norm_mem_g, wq_x, wkv_x, wo_x, norm_ffn_g, w_ffn_in, w_ffn_out, final_norm_g):
    batch, seq, d = x.shape
    n_mem = mem.shape[1]
    depth = w_in.shape[0]
    assert depth == 1 and d == A_HEADS * A_HEAD_DIM == B_HEADS * LANES and B_KEY_DIM == LANES
    tm = min(512, seq)
    tq = MOBA_QUERY_BLOCKS * MOBA_BLOCK
    assert seq % tq == 0 and seq % tm == 0 and w_in.shape[2] == 9 * d
    lt = min(1024, seq)
    x2d = x.reshape(batch * seq, d)
    cos_t, sin_t, cos_tt, sin_tt = _rope_tables(positions)
    row = lambda g: g.reshape(1, -1).astype(F32)
    bf = lambda w: w.astype(BF16)

    w_in16 = bf(w_in[0])
    wt_qv = jnp.stack([w_in16[:, :d].T, w_in16[:, 2 * d:3 * d].T])
    w_rest = jnp.concatenate([w_in16[:, d:2 * d], w_in16[:, 3 * d:]], axis=1)

    kvm = _memkv(mem.reshape(batch * n_mem, d), row(norm_mem_g[0]), bf(wkv_x[0]), n_mem)
    z, qt, vt, sel = _inproj(x2d, row(norm_mix_g[0]), w_rest, wt_qv, cos_t, sin_t, cos_tt, sin_tt, tm, seq)
    attn = _moba(z, qt, vt, sel, batch, seq, tq)
    hg = _hgrn(z, hgrn_lb_logits.astype(F32), row(hgrn_norm_g[0]), batch, seq, lt, HGRN_GROUP)
    h = _merge(x2d, attn, hg, z, bf(w_br_attn[0]), bf(w_br_hgrn[0]), bf(w_out[0]), row(norm_x_g[0]),
               bf(wq_x[0]), kvm, bf(wo_x[0]), batch, seq, n_mem, tm)
    out = _ffn(h, row(norm_ffn_g[0]), bf(w_ffn_in[0]), bf(w_ffn_out[0]), row(final_norm_g), tm)
    return out.reshape(batch, seq, d)
```

```python
import functools

import jax
import jax.numpy as jnp
from jax import lax
from jax.experimental import pallas as pl
from jax.experimental.pallas import tpu as pltpu

F32 = jnp.float32
BF16 = jnp.bfloat16

EPS = 1e-6
NEG = -1e30
ROPE_THETA = 10000.0

A_HEADS = 16
A_HEAD_DIM = 64
MOBA_BLOCK = 256
MOBA_TOPK = 3
MOBA_QUERY_BLOCKS = 4
MOBA_LOOKAHEAD = 3
B_HEADS = 8
B_KEY_DIM = 128
X_HEADS = 4
HGRN_CHUNK = 64
HGRN_SUB = 16
HGRN_GROUP = 4
HGRN_UNROLL = 4

LANES = 128
SUM_ROWS = 16
LOG2_E = 1.4426950408889634
VMEM_LIMIT = 56 * 1024 * 1024


def _params(sem, flags=None):
    return pltpu.CompilerParams(dimension_semantics=sem, vmem_limit_bytes=VMEM_LIMIT, flags=flags)


def _rms(x, g):
    return x * lax.rsqrt(jnp.mean(x * x, axis=-1, keepdims=True) + EPS) * g


def _dot(a, b):
    return jnp.dot(a, b, preferred_element_type=F32)


def _dot_nt(a, b):
    return lax.dot_general(a, b, (((1,), (1,)), ((), ())), preferred_element_type=F32)


def _sigmoid(x):
    return 1.0 / (1.0 + jnp.exp(-x))


def _memkv_kernel(mem_ref, g_ref, w_ref, o_ref):
    y = _rms(mem_ref[...], g_ref[...])
    o_ref[...] = _dot(y.astype(BF16), w_ref[...]).astype(BF16)


def _memkv(mem2d, g, w, n_mem):
    rows, d = mem2d.shape
    return pl.pallas_call(
        _memkv_kernel,
        grid=(rows // n_mem,),
        in_specs=[
            pl.BlockSpec((n_mem, d), lambda b: (b, 0)),
            pl.BlockSpec((1, d), lambda b: (0, 0)),
            pl.BlockSpec(w.shape, lambda b: (0, 0)),
        ],
        out_specs=pl.BlockSpec((n_mem, w.shape[1]), lambda b: (b, 0)),
        out_shape=jax.ShapeDtypeStruct((rows, w.shape[1]), BF16),
        compiler_params=_params(("parallel",)),
        name="memkv",
    )(mem2d, g, w)


def _inproj_kernel(x_ref, g_ref, w_ref, wt_ref, cos_ref, sin_ref, cost_ref, sint_ref, z_ref, qt_ref, vt_ref, *, tm):
    hd, half = A_HEAD_DIM, A_HEAD_DIM // 2
    d = x_ref.shape[1]
    xn = _rms(x_ref[...], g_ref[...]).astype(BF16)

    acc = _dot_nt(wt_ref[0], xn)
    cos = cost_ref[...]
    sin = sint_ref[...]
    scale = hd ** -0.5 * LOG2_E
    for h in range(d // hd):
        t1 = acc[h * hd:h * hd + half, :]
        t2 = acc[h * hd + half:(h + 1) * hd, :]
        qt_ref[h * hd:h * hd + half, :] = ((t1 * cos - t2 * sin) * scale).astype(BF16)
        qt_ref[h * hd + half:(h + 1) * hd, :] = ((t2 * cos + t1 * sin) * scale).astype(BF16)

    acc = _dot(xn, w_ref[:, :d])
    cos = cos_ref[...]
    sin = sin_ref[...]
    lane = lax.broadcasted_iota(jnp.int32, cos.shape, 1)
    low = (lane % hd) < half
    for c in range(d // LANES):
        t = acc[:, c * LANES:(c + 1) * LANES]
        rot = jnp.where(low, pltpu.roll(t, LANES - half, 1), pltpu.roll(t, half, 1))
        z_ref[:, c * LANES:(c + 1) * LANES] = (t * cos + rot * sin).astype(BF16)

    acc = _dot_nt(wt_ref[1], xn).astype(BF16)
    for n in range(tm // MOBA_BLOCK):
        vt_ref[n] = acc[:, n * MOBA_BLOCK:(n + 1) * MOBA_BLOCK]

    for s in range(1, w_ref.shape[1] // d):
        z_ref[:, s * d:(s + 1) * d] = _dot(xn, w_ref[:, s * d:(s + 1) * d]).astype(BF16)


def _inproj(x2d, g, w, wt, cos_t, sin_t, cos_tt, sin_tt, tm):
    t, d = x2d.shape
    blk = MOBA_BLOCK
    full = lambda i: (0, 0)
    return pl.pallas_call(
        functools.partial(_inproj_kernel, tm=tm),
        grid=(t // tm,),
        in_specs=[
            pl.BlockSpec((tm, d), lambda i: (i, 0)),
            pl.BlockSpec((1, d), full),
            pl.BlockSpec(w.shape, full, pipeline_mode=pl.Buffered(1)),
            pl.BlockSpec(wt.shape, lambda i: (0, 0, 0), pipeline_mode=pl.Buffered(1)),
            pl.BlockSpec((tm, LANES), lambda i: (i, 0)),
            pl.BlockSpec((tm, LANES), lambda i: (i, 0)),
            pl.BlockSpec((A_HEAD_DIM // 2, tm), lambda i: (0, i)),
            pl.BlockSpec((A_HEAD_DIM // 2, tm), lambda i: (0, i)),
        ],
        out_specs=[
            pl.BlockSpec((tm, w.shape[1]), lambda i: (i, 0)),
            pl.BlockSpec((d, tm), lambda i: (0, i)),
            pl.BlockSpec((tm // blk, d, blk), lambda i: (i, 0, 0)),
        ],
        out_shape=[
            jax.ShapeDtypeStruct((t, w.shape[1]), BF16),
            jax.ShapeDtypeStruct((d, t), BF16),
            jax.ShapeDtypeStruct((t // blk, d, blk), BF16),
        ],
        compiler_params=_params(("parallel",)),
        name="inproj",
    )(x2d, g, w, wt, cos_t, sin_t, cos_tt, sin_tt)


def _moba_kernel(q_ref, k_ref, v_ref, o_ref, km_ref, sel_ref, m_ref, acc_ref, s_ref, *, nb, blk, tq):
    i = pl.program_id(2)
    hd = A_HEAD_DIM
    halves = tq // blk

    @pl.when(i == 0)
    def _prep():
        for n in range(nb):
            km_ref[n:n + 1, :] = jnp.mean(k_ref[n * blk:(n + 1) * blk, :].astype(F32), axis=0, keepdims=True)

    q_t = q_ref[...]
    row = lax.broadcasted_iota(jnp.int32, q_t.shape, 0)
    zero = jnp.zeros_like(q_t)
    qz = [jnp.where(row < hd, q_t, zero), jnp.where(row >= hd, q_t, zero)]

    km = km_ref[...]
    km_hi = km.astype(BF16)
    km_lo = (km - km_hi.astype(F32)).astype(BF16)
    blk_id = lax.broadcasted_iota(jnp.int32, (nb, tq), 0)
    q_blk = i * halves + lax.broadcasted_iota(jnp.int32, (nb, tq), 1) // blk
    gates = [_dot(km_hi, qz[h]) + _dot(km_lo, qz[h]) for h in range(2)]

    def select(h):
        def run():
            g = jnp.where(blk_id < q_blk, gates[h], -jnp.inf)
            sel = jnp.zeros((nb, tq), F32)
            for _ in range(MOBA_TOPK):
                mx = jnp.max(g, axis=0, keepdims=True)
                first = jnp.min(jnp.where(g == mx, blk_id, nb), axis=0, keepdims=True)
                pick = jnp.logical_and(blk_id == first, mx > -jnp.inf)
                sel = jnp.where(pick, 1.0, sel)
                g = jnp.where(blk_id == first, -jnp.inf, g)
            sel_ref[h] = sel
        return run

    m_ref[...] = jnp.full(m_ref.shape, NEG, F32)
    acc_ref[...] = jnp.zeros(acc_ref.shape, F32)
    ones = jnp.ones((SUM_ROWS, blk), BF16)

    groups = [(h, c) for h in range(2) for c in range(halves)]

    def score(slot, j, g):
        def run():
            h, c = groups[g]
            kj = k_ref[pl.ds(pl.multiple_of(j * blk, blk), blk), :]
            s_ref[slot, g] = _dot(kj, qz[h][:, c * blk:(c + 1) * blk])
        return run

    def finish(slot, j, g, own):
        def run():
            h, c = groups[g]
            cs = slice(c * blk, (c + 1) * blk)
            s = s_ref[slot, g]
            m_old = m_ref[h:h + 1, cs]
            if own:
                s = jnp.where(causal, s, NEG)
                m_new = jnp.maximum(m_old, jnp.max(s, axis=0, keepdims=True))
                m_use = m_new
            else:
                on = sel_ref[h, pl.ds(j, 1), cs] > 0.0
                m_new = jnp.where(on, jnp.maximum(m_old, jnp.max(s, axis=0, keepdims=True)), m_old)
                m_use = jnp.where(on, m_new, -NEG)
            p = jnp.exp2(s - m_use).astype(BF16)
            pv = _dot(jnp.concatenate([v_ref[j, h * hd:(h + 1) * hd, :], ones], axis=0), p)
            acc_ref[h, :, cs] = jnp.exp2(m_old - m_new) * acc_ref[h, :, cs] + pv
            m_ref[h:h + 1, cs] = m_new
        return run

    def overlap(scores, finishes):
        for run in scores[:MOBA_LOOKAHEAD]:
            run()
        rest = scores[MOBA_LOOKAHEAD:]
        for run in finishes:
            run()
            if rest:
                rest.pop(0)()
        for run in rest:
            run()

    everyone = range(len(groups))
    overlap([score(0, 0, g) for g in everyone], [select(0), select(1)])

    def body(jj, carry):
        j0 = halves * jj
        for t in range(halves):
            overlap([score((t + 1) % 2, j0 + t + 1, g) for g in everyone],
                    [finish(t % 2, j0 + t, g, False) for g in everyone])
        return carry

    lax.fori_loop(0, i, body, 0)

    causal = (lax.broadcasted_iota(jnp.int32, (blk, blk), 0)
              <= lax.broadcasted_iota(jnp.int32, (blk, blk), 1))
    base = i * halves
    for t in range(halves):
        later = [g for g in everyone if groups[g][1] > t]
        overlap([score((t + 1) % 2, base + t + 1, g) for g in later],
                [finish(t % 2, base + t, g, groups[g][1] == t) for g in everyone if groups[g][1] >= t])

    out_t = jnp.concatenate([acc_ref[h, :hd, :] / acc_ref[h, hd:hd + 1, :] for h in range(2)], axis=0)
    o_ref[...] = out_t.T.astype(BF16)


def _moba(z, qt, vt, batch, seq, tq):
    blk = MOBA_BLOCK
    nb = seq // blk
    nq = seq // tq
    d = qt.shape[0]
    pairs = d // LANES
    return pl.pallas_call(
        functools.partial(_moba_kernel, nb=nb, blk=blk, tq=tq),
        grid=(batch, pairs, nq),
        in_specs=[
            pl.BlockSpec((LANES, tq), lambda b, p, i: (p, b * nq + i)),
            pl.BlockSpec((seq, LANES), lambda b, p, i: (b, p)),
            pl.BlockSpec((nb, LANES, blk), lambda b, p, i: (b, p, 0)),
        ],
        out_specs=pl.BlockSpec((tq, LANES), lambda b, p, i: (b * nq + i, p)),
        out_shape=jax.ShapeDtypeStruct((batch * seq, d), BF16),
        scratch_shapes=[
            pltpu.VMEM((nb, LANES), F32),
            pltpu.VMEM((2, nb, tq), F32),
            pltpu.VMEM((2, tq), F32),
            pltpu.VMEM((2, A_HEAD_DIM + SUM_ROWS, tq), F32),
            pltpu.VMEM((2, 2 * (tq // blk), blk, blk), F32),
        ],
        compiler_params=_params(("parallel", "parallel", "arbitrary")),
        name="moba",
    )(qt, z, vt)


def _hgrn_kernel(q_ref, f_ref, i_ref, og_ref, lbl_ref, ng_ref, o_ref, st_ref, *, lt, group):
    c_len, sub = HGRN_CHUNK, HGRN_SUB
    n_sub = c_len // sub
    dk = B_KEY_DIM
    heads = [slice(g * LANES, (g + 1) * LANES) for g in range(group)]

    @pl.when(pl.program_id(2) == 0)
    def _():
        st_ref[...] = jnp.zeros(st_ref.shape, F32)

    lg = lbl_ref[...]
    e = jnp.exp(lg - jnp.max(lg, axis=0, keepdims=True))
    lb = e[0:1, :] / jnp.sum(e, axis=0, keepdims=True)
    ng = ng_ref[...]

    tri = (lax.broadcasted_iota(jnp.int32, (c_len, c_len), 0)
           >= lax.broadcasted_iota(jnp.int32, (c_len, c_len), 1)).astype(BF16)
    width = group * dk
    half = sub // 2
    n_half = c_len // half
    rowi = lax.broadcasted_iota(jnp.int32, (n_half, half, width), 1)
    same_sub = (lax.broadcasted_iota(jnp.int32, (n_sub * half, n_sub * half), 0) // half
                == lax.broadcasted_iota(jnp.int32, (n_sub * half, n_sub * half), 1) // half)

    def step(c, states):
        rows = pl.ds(pl.multiple_of(c * c_len, c_len), c_len)
        qb = q_ref[rows, :].astype(F32)
        fb = f_ref[rows, :].astype(F32)
        v = i_ref[rows, :].astype(F32)
        og = og_ref[rows, :].astype(F32)

        q = qb * _sigmoid(qb) * (dk ** -0.5)
        fg = lb + (1.0 - lb) * _sigmoid(fb)
        logf = jnp.log2(fg)
        k = 1.0 - fg

        hi = logf.astype(BF16)
        r1 = logf - hi.astype(F32)
        mid = r1.astype(BF16)
        lo = (r1 - mid.astype(F32)).astype(BF16)
        b = _dot(tri, hi) + _dot(tri, mid) + _dot(tri, lo)

        qe = (q * jnp.exp2(b)).astype(BF16)
        o_inter = [_dot_nt(qe[:, hs], states[g].astype(BF16)) for g, hs in enumerate(heads)]

        v16 = v.astype(BF16)
        pair_a = []
        for s_i in range(1, n_sub):
            lo_r = s_i * sub
            ref_b = b[lo_r - 1:lo_r, :]
            qt = (q[lo_r:lo_r + sub, :] * jnp.exp2(b[lo_r:lo_r + sub, :] - ref_b)).astype(BF16)
            kt = (k[:lo_r, :] * jnp.exp2(ref_b - b[:lo_r, :])).astype(BF16)
            pair_a.append([_dot_nt(qt[:, hs], kt[:, hs]).astype(BF16) for hs in heads])
        pair_o = [[_dot(pair_a[s_i - 1][g], v16[:s_i * sub, hs]) for s_i in range(1, n_sub)]
                  for g, hs in enumerate(heads)]

        b3 = b.reshape(n_sub, sub, width)
        ref_h = b3[:, half - 1:half, :]
        q_h = q.reshape(n_sub, sub, width)[:, half:, :] * jnp.exp2(b3[:, half:, :] - ref_h)
        k_h = k.reshape(n_sub, sub, width)[:, :half, :] * jnp.exp2(ref_h - b3[:, :half, :])
        q_h = q_h.reshape(n_sub * half, width).astype(BF16)
        k_h = k_h.reshape(n_sub * half, width).astype(BF16)
        v_h = v.reshape(n_sub, sub, width)[:, :half, :].reshape(n_sub * half, width).astype(BF16)
        half_a = [jnp.where(same_sub, _dot_nt(q_h[:, hs], k_h[:, hs]), 0.0).astype(BF16) for hs in heads]
        half_o = [_dot(half_a[g], v_h[:, hs]) for g, hs in enumerate(heads)]

        b_last = b[c_len - 1:c_len, :]
        kl = (k * jnp.exp2(b_last - b)).astype(BF16)
        decay = jnp.exp2(b_last)
        new_states = [states[g] * decay[:, hs] + _dot(v[:, hs].T.astype(BF16), kl[:, hs])
                      for g, hs in enumerate(heads)]

        b8 = b.reshape(n_half, half, width)
        q8 = q.reshape(n_half, half, width)
        k8 = k.reshape(n_half, half, width)
        v8 = v.reshape(n_half, half, width)
        od = [jnp.zeros((n_half, half, dk), F32) for _ in heads]
        for s_o in range(half):
            dec = jnp.exp2(jnp.where(rowi >= s_o, b8 - b8[:, s_o:s_o + 1, :], NEG))
            gq = q8 * dec * k8[:, s_o:s_o + 1, :]
            for g, hs in enumerate(heads):
                od[g] = od[g] + jnp.sum(gq[:, :, hs], axis=-1, keepdims=True) * v8[:, s_o:s_o + 1, hs]

        gate = og * _sigmoid(og)
        for g, hs in enumerate(heads):
            o = o_inter[g] + jnp.concatenate([jnp.zeros((sub, dk), F32)] + pair_o[g], axis=0)
            second = jnp.concatenate([jnp.zeros((n_sub, half, dk), F32), half_o[g].reshape(n_sub, half, dk)], axis=1)
            o = o + second.reshape(c_len, dk) + od[g].reshape(c_len, dk)
            o = o * lax.rsqrt(jnp.mean(o * o, axis=-1, keepdims=True) + EPS) * ng[:, hs]
            o_ref[rows, hs] = (o * gate[:, hs]).astype(BF16)
        return new_states

    def trip(n, carry):
        states = [st_ref[g] for g in range(group)]
        for u in range(HGRN_UNROLL):
            states = step(n * HGRN_UNROLL + u, states)
        for g in range(group):
            st_ref[g] = states[g]
        return carry

    lax.fori_loop(0, lt // (c_len * HGRN_UNROLL), trip, 0)


def _hgrn(z, lb_logits, norm_g, batch, seq, lt, group):
    nt = seq // lt
    width = group * LANES
    d = B_HEADS * LANES
    seg = lambda off: pl.BlockSpec((lt, width), lambda b, hh, t: (b * nt + t, off * (d // width) + hh))
    return pl.pallas_call(
        functools.partial(_hgrn_kernel, lt=lt, group=group),
        grid=(batch, B_HEADS // group, nt),
        in_specs=[
            seg(1), seg(2), seg(3), seg(4),
            pl.BlockSpec((lb_logits.shape[0], width), lambda b, hh, t: (0, hh)),
            pl.BlockSpec((1, width), lambda b, hh, t: (0, hh)),
        ],
        out_specs=pl.BlockSpec((lt, width), lambda b, hh, t: (b * nt + t, hh)),
        out_shape=jax.ShapeDtypeStruct((batch * seq, d), BF16),
        scratch_shapes=[pltpu.VMEM((group, LANES, B_KEY_DIM), F32)],
        compiler_params=_params(("parallel", "parallel", "arbitrary")),
        name="hgrn",
    )(z, z, z, z, lb_logits, norm_g)


def _merge_kernel(x_ref, a_ref, hg_ref, ga_ref, gb_ref, wa_ref, wb_ref, wo_ref, gx_ref, wq_ref,
                  km_ref, vm_ref, wox_ref, o_ref):
    ya = _dot(a_ref[...], wa_ref[...])
    yb = _dot(hg_ref[...], wb_ref[...])
    mg = _sigmoid(ga_ref[...].astype(F32)) * ya + _sigmoid(gb_ref[...].astype(F32)) * yb
    h = x_ref[...] + _dot(mg.astype(BF16), wo_ref[...])

    hn = _rms(h, gx_ref[...]).astype(BF16)
    q = _dot(hn, wq_ref[...])
    d = q.shape[1]
    xd = d // X_HEADS
    outs = []
    for hh in range(X_HEADS):
        sl = slice(hh * xd, (hh + 1) * xd)
        s = _dot_nt(q[:, sl].astype(BF16), km_ref[:, sl]) * (xd ** -0.5)
        p = jnp.exp(s - jnp.max(s, axis=-1, keepdims=True))
        oh = _dot(p.astype(BF16), vm_ref[:, sl])
        outs.append(oh / jnp.sum(p, axis=-1, keepdims=True))
    o = jnp.concatenate(outs, axis=-1)
    o_ref[...] = h + _dot(o.astype(BF16), wox_ref[...])


def _merge(x2d, attn, hg, z, wa, wb, wo, gx, wq, kvm, wox, batch, seq, n_mem, tm):
    d = x2d.shape[1]
    nt = seq // tm
    gcol = (z.shape[1] - 2 * d) // d
    row = lambda b, t: (b * nt + t, 0)
    full = lambda b, t: (0, 0)
    wspec = pl.BlockSpec((d, d), full)
    return pl.pallas_call(
        _merge_kernel,
        grid=(batch, nt),
        in_specs=[
            pl.BlockSpec((tm, d), row),
            pl.BlockSpec((tm, d), row),
            pl.BlockSpec((tm, d), row),
            pl.BlockSpec((tm, d), lambda b, t: (b * nt + t, gcol)),
            pl.BlockSpec((tm, d), lambda b, t: (b * nt + t, gcol + 1)),
            wspec, wspec, wspec,
            pl.BlockSpec((1, d), full),
            wspec,
            pl.BlockSpec((n_mem, d), lambda b, t: (b, 0)),
            pl.BlockSpec((n_mem, d), lambda b, t: (b, 1)),
            wspec,
        ],
        out_specs=pl.BlockSpec((tm, d), row),
        out_shape=jax.ShapeDtypeStruct(x2d.shape, F32),
        compiler_params=_params(("parallel", "parallel")),
        name="merge",
    )(x2d, attn, hg, z, z, wa, wb, wo, gx, wq, kvm, kvm, wox)


def _ffn_kernel(h_ref, g_ref, wi_ref, wo_ref, gf_ref, o_ref, *, hidden, bounds):
    h = h_ref[...]
    hn = _rms(h, g_ref[...]).astype(BF16)
    acc = h
    for lo, hi in bounds:
        gate = _dot(hn, wi_ref[:, lo:hi])
        up = _dot(hn, wi_ref[:, hidden + lo:hidden + hi])
        act = (gate * _sigmoid(gate) * up).astype(BF16)
        acc = acc + _dot(act, wo_ref[lo:hi, :])
    o_ref[...] = _rms(acc, gf_ref[...])


def _ffn(h2d, g, wi, wo, gf, tm):
    t, d = h2d.shape
    hidden = wo.shape[0]
    step = 1024
    bounds = tuple((lo, min(lo + step, hidden)) for lo in range(0, hidden, step))
    full = lambda i: (0, 0)
    return pl.pallas_call(
        functools.partial(_ffn_kernel, hidden=hidden, bounds=bounds),
        grid=(t // tm,),
        in_specs=[
            pl.BlockSpec((tm, d), lambda i: (i, 0)),
            pl.BlockSpec((1, d), full),
            pl.BlockSpec(wi.shape, full, pipeline_mode=pl.Buffered(1)),
            pl.BlockSpec(wo.shape, full, pipeline_mode=pl.Buffered(1)),
            pl.BlockSpec((1, d), full),
        ],
        out_specs=pl.BlockSpec((tm, d), lambda i: (i, 0)),
        out_shape=jax.ShapeDtypeStruct(h2d.shape, F32),
        compiler_params=_params(("parallel",)),
        name="ffn",
    )(h2d, g, wi, wo, gf)


def _rope_tables(positions):
    half = A_HEAD_DIM // 2
    inv_freq = jnp.power(ROPE_THETA, -jnp.arange(0, A_HEAD_DIM, 2, dtype=F32) / A_HEAD_DIM)
    pos = positions.astype(F32).reshape(-1)
    lane = jnp.arange(LANES)
    ang = pos[:, None] * inv_freq[lane % half][None, :]
    sign = jnp.where((lane % A_HEAD_DIM) < half, -1.0, 1.0).astype(F32)
    ang_t = inv_freq[:, None] * pos[None, :]
    return jnp.cos(ang), jnp.sin(ang) * sign, jnp.cos(ang_t), jnp.sin(ang_t)


def kernel(x, mem, positions, norm_mix_g, w_in, hgrn_lb_logits, hgrn_norm_g, w_br_attn, w_br_hgrn, w_out,
           norm_x_g, norm_mem_g, wq_x, wkv_x, wo_x, norm_ffn_g, w_ffn_in, w_ffn_out, final_norm_g):
    batch, seq, d = x.shape
    n_mem = mem.shape[1]
    depth = w_in.shape[0]
    assert depth == 1 and d == A_HEADS * A_HEAD_DIM == B_HEADS * LANES and B_KEY_DIM == LANES
    tm = min(512, seq)
    tq = MOBA_QUERY_BLOCKS * MOBA_BLOCK
    assert seq % tq == 0 and seq % tm == 0 and w_in.shape[2] == 9 * d
    lt = min(1024, seq)
    x2d = x.reshape(batch * seq, d)
    cos_t, sin_t, cos_tt, sin_tt = _rope_tables(positions)
    row = lambda g: g.reshape(1, -1).astype(F32)
    bf = lambda w: w.astype(BF16)

    w_in16 = bf(w_in[0])
    wt_qv = jnp.stack([w_in16[:, :d].T, w_in16[:, 2 * d:3 * d].T])
    w_rest = jnp.concatenate([w_in16[:, d:2 * d], w_in16[:, 3 * d:]], axis=1)

    kvm = _memkv(mem.reshape(batch * n_mem, d), row(norm_mem_g[0]), bf(wkv_x[0]), n_mem)
    z, qt, vt = _inproj(x2d, row(norm_mix_g[0]), w_rest, wt_qv, cos_t, sin_t, cos_tt, sin_tt, tm)
    attn = _moba(z, qt, vt, batch, seq, tq)
    hg = _hgrn(z, hgrn_lb_logits.astype(F32), row(hgrn_norm_g[0]), batch, seq, lt, HGRN_GROUP)
    h = _merge(x2d, attn, hg, z, bf(w_br_attn[0]), bf(w_br_hgrn[0]), bf(w_out[0]), row(norm_x_g[0]),
               bf(wq_x[0]), kvm, bf(wo_x[0]), batch, seq, n_mem, tm)
    out = _ffn(h, row(norm_ffn_g[0]), bf(w_ffn_in[0]), bf(w_ffn_out[0]), row(final_norm_g), tm)
    return out.reshape(batch, seq, d)
```

```python
import functools

import jax
import jax.numpy as jnp
from jax import lax
from jax.experimental import pallas as pl
from jax.experimental.pallas import tpu as pltpu

F32 = jnp.float32
BF16 = jnp.bfloat16

EPS = 1e-6
NEG = -1e30
ROPE_THETA = 10000.0

A_HEADS = 16
A_HEAD_DIM = 64
MOBA_BLOCK = 256
MOBA_TOPK = 3
MOBA_QUERY_BLOCKS = 4
MOBA_LOOKAHEAD = 3
B_HEADS = 8
B_KEY_DIM = 128
X_HEADS = 4
HGRN_CHUNK = 64
HGRN_SUB = 16
HGRN_GROUP = 8
HGRN_UNROLL = 2

LANES = 128
SUM_ROWS = 16
LOG2_E = 1.4426950408889634
VMEM_LIMIT = 56 * 1024 * 1024


def _params(sem):
    return pltpu.CompilerParams(dimension_semantics=sem, vmem_limit_bytes=VMEM_LIMIT)


def _rms(x, g):
    return x * lax.rsqrt(jnp.mean(x * x, axis=-1, keepdims=True) + EPS) * g


def _dot(a, b):
    return jnp.dot(a, b, preferred_element_type=F32)


def _dot_nt(a, b):
    return lax.dot_general(a, b, (((1,), (1,)), ((), ())), preferred_element_type=F32)


def _sigmoid(x):
    return 1.0 / (1.0 + jnp.exp(-x))


def _memkv_kernel(mem_ref, g_ref, w_ref, o_ref):
    y = _rms(mem_ref[...], g_ref[...])
    o_ref[...] = _dot(y.astype(BF16), w_ref[...]).astype(BF16)


def _memkv(mem2d, g, w, n_mem):
    rows, d = mem2d.shape
    return pl.pallas_call(
        _memkv_kernel,
        grid=(rows // n_mem,),
        in_specs=[
            pl.BlockSpec((n_mem, d), lambda b: (b, 0)),
            pl.BlockSpec((1, d), lambda b: (0, 0)),
            pl.BlockSpec(w.shape, lambda b: (0, 0)),
        ],
        out_specs=pl.BlockSpec((n_mem, w.shape[1]), lambda b: (b, 0)),
        out_shape=jax.ShapeDtypeStruct((rows, w.shape[1]), BF16),
        compiler_params=_params(("parallel",)),
        name="memkv",
    )(mem2d, g, w)


def _inproj_kernel(x_ref, g_ref, w_ref, wt_ref, cos_ref, sin_ref, cost_ref, sint_ref, z_ref, qt_ref, vt_ref, *, tm):
    hd, half = A_HEAD_DIM, A_HEAD_DIM // 2
    d = x_ref.shape[1]
    xn = _rms(x_ref[...], g_ref[...]).astype(BF16)

    acc = _dot_nt(wt_ref[0], xn)
    cos = cost_ref[...]
    sin = sint_ref[...]
    scale = hd ** -0.5 * LOG2_E
    for h in range(d // hd):
        t1 = acc[h * hd:h * hd + half, :]
        t2 = acc[h * hd + half:(h + 1) * hd, :]
        qt_ref[h * hd:h * hd + half, :] = ((t1 * cos - t2 * sin) * scale).astype(BF16)
        qt_ref[h * hd + half:(h + 1) * hd, :] = ((t2 * cos + t1 * sin) * scale).astype(BF16)

    acc = _dot(xn, w_ref[:, d:2 * d])
    cos = cos_ref[...]
    sin = sin_ref[...]
    lane = lax.broadcasted_iota(jnp.int32, cos.shape, 1)
    low = (lane % hd) < half
    for c in range(d // LANES):
        t = acc[:, c * LANES:(c + 1) * LANES]
        rot = jnp.where(low, pltpu.roll(t, LANES - half, 1), pltpu.roll(t, half, 1))
        z_ref[:, c * LANES:(c + 1) * LANES] = (t * cos + rot * sin).astype(BF16)

    acc = _dot_nt(wt_ref[1], xn).astype(BF16)
    for n in range(tm // MOBA_BLOCK):
        vt_ref[n] = acc[:, n * MOBA_BLOCK:(n + 1) * MOBA_BLOCK]

    for s in range(3, w_ref.shape[1] // d):
        z_ref[:, (s - 2) * d:(s - 1) * d] = _dot(xn, w_ref[:, s * d:(s + 1) * d]).astype(BF16)


def _inproj(x2d, g, w, wt, cos_t, sin_t, cos_tt, sin_tt, tm):
    t, d = x2d.shape
    blk = MOBA_BLOCK
    full = lambda i: (0, 0)
    return pl.pallas_call(
        functools.partial(_inproj_kernel, tm=tm),
        grid=(t // tm,),
        in_specs=[
            pl.BlockSpec((tm, d), lambda i: (i, 0)),
            pl.BlockSpec((1, d), full),
            pl.BlockSpec(w.shape, full, pipeline_mode=pl.Buffered(1)),
            pl.BlockSpec(wt.shape, lambda i: (0, 0, 0), pipeline_mode=pl.Buffered(1)),
            pl.BlockSpec((tm, LANES), lambda i: (i, 0)),
            pl.BlockSpec((tm, LANES), lambda i: (i, 0)),
            pl.BlockSpec((A_HEAD_DIM // 2, tm), lambda i: (0, i)),
            pl.BlockSpec((A_HEAD_DIM // 2, tm), lambda i: (0, i)),
        ],
        out_specs=[
            pl.BlockSpec((tm, w.shape[1] - 2 * d), lambda i: (i, 0)),
            pl.BlockSpec((d, tm), lambda i: (0, i)),
            pl.BlockSpec((tm // blk, d, blk), lambda i: (i, 0, 0)),
        ],
        out_shape=[
            jax.ShapeDtypeStruct((t, w.shape[1] - 2 * d), BF16),
            jax.ShapeDtypeStruct((d, t), BF16),
            jax.ShapeDtypeStruct((t // blk, d, blk), BF16),
        ],
        compiler_params=_params(("parallel",)),
        name="inproj",
    )(x2d, g, w, wt, cos_t, sin_t, cos_tt, sin_tt)


def _moba_kernel(q_ref, k_ref, v_ref, o_ref, km_ref, sel_ref, m_ref, acc_ref, s_ref, *, nb, blk, tq):
    i = pl.program_id(2)
    hd = A_HEAD_DIM
    halves = tq // blk

    @pl.when(i == 0)
    def _prep():
        for n in range(nb):
            km_ref[n:n + 1, :] = jnp.mean(k_ref[n * blk:(n + 1) * blk, :].astype(F32), axis=0, keepdims=True)

    q_t = q_ref[...]
    row = lax.broadcasted_iota(jnp.int32, q_t.shape, 0)
    zero = jnp.zeros_like(q_t)
    qz = [jnp.where(row < hd, q_t, zero), jnp.where(row >= hd, q_t, zero)]

    km = km_ref[...]
    km_hi = km.astype(BF16)
    km_lo = (km - km_hi.astype(F32)).astype(BF16)
    blk_id = lax.broadcasted_iota(jnp.int32, (nb, tq), 0)
    q_blk = i * halves + lax.broadcasted_iota(jnp.int32, (nb, tq), 1) // blk
    gates = [_dot(km_hi, qz[h]) + _dot(km_lo, qz[h]) for h in range(2)]

    def select(h):
        def run():
            g = jnp.where(blk_id < q_blk, gates[h], -jnp.inf)
            sel = jnp.zeros((nb, tq), F32)
            for _ in range(MOBA_TOPK):
                mx = jnp.max(g, axis=0, keepdims=True)
                first = jnp.min(jnp.where(g == mx, blk_id, nb), axis=0, keepdims=True)
                first = jnp.where(mx > -jnp.inf, first, -1)
                pick = blk_id == first
                sel = jnp.where(pick, 1.0, sel)
                g = jnp.where(pick, -jnp.inf, g)
            sel_ref[h] = sel
        return run

    m_ref[...] = jnp.full(m_ref.shape, NEG, F32)
    acc_ref[...] = jnp.zeros(acc_ref.shape, F32)
    ones = jnp.ones((SUM_ROWS, blk), BF16)

    groups = [(h, c) for h in range(2) for c in range(halves)]

    def score(slot, j, g):
        def run():
            h, c = groups[g]
            kj = k_ref[pl.ds(pl.multiple_of(j * blk, blk), blk), :]
            s_ref[slot, g] = _dot(kj, qz[h][:, c * blk:(c + 1) * blk])
        return run

    def finish(slot, j, g, own):
        def run():
            h, c = groups[g]
            cs = slice(c * blk, (c + 1) * blk)
            s = s_ref[slot, g]
            m_old = m_ref[h:h + 1, cs]
            if own:
                s = jnp.where(causal, s, NEG)
                m_new = jnp.maximum(m_old, jnp.max(s, axis=0, keepdims=True))
                m_use = m_new
            else:
                on = sel_ref[h, pl.ds(j, 1), cs] > 0.0
                m_new = jnp.where(on, jnp.maximum(m_old, jnp.max(s, axis=0, keepdims=True)), m_old)
                m_use = jnp.where(on, m_new, -NEG)
            p = jnp.exp2(s - m_use).astype(BF16)
            pv = _dot(jnp.concatenate([v_ref[j, h * hd:(h + 1) * hd, :], ones], axis=0), p)
            acc_ref[h, :, cs] = jnp.exp2(m_old - m_new) * acc_ref[h, :, cs] + pv
            m_ref[h:h + 1, cs] = m_new
        return run

    def overlap(scores, finishes):
        for run in scores[:MOBA_LOOKAHEAD]:
            run()
        rest = scores[MOBA_LOOKAHEAD:]
        for run in finishes:
            run()
            if rest:
                rest.pop(0)()
        for run in rest:
            run()

    everyone = range(len(groups))
    overlap([score(0, 0, g) for g in everyone], [select(0), select(1)])

    def body(jj, carry):
        j0 = halves * jj
        for t in range(halves):
            overlap([score((t + 1) % 2, j0 + t + 1, g) for g in everyone],
                    [finish(t % 2, j0 + t, g, False) for g in everyone])
        return carry

    lax.fori_loop(0, i, body, 0)

    causal = (lax.broadcasted_iota(jnp.int32, (blk, blk), 0)
              <= lax.broadcasted_iota(jnp.int32, (blk, blk), 1))
    base = i * halves
    for t in range(halves):
        later = [g for g in everyone if groups[g][1] > t]
        overlap([score((t + 1) % 2, base + t + 1, g) for g in later],
                [finish(t % 2, base + t, g, groups[g][1] == t) for g in everyone if groups[g][1] >= t])

    out_t = jnp.concatenate([acc_ref[h, :hd, :] * (1.0 / acc_ref[h, hd:hd + 1, :]) for h in range(2)], axis=0)
    o_ref[...] = out_t.T.astype(BF16)


def _moba(z, qt, vt, batch, seq, tq):
    blk = MOBA_BLOCK
    nb = seq // blk
    nq = seq // tq
    d = qt.shape[0]
    pairs = d // LANES
    return pl.pallas_call(
        functools.partial(_moba_kernel, nb=nb, blk=blk, tq=tq),
        grid=(batch, pairs, nq),
        in_specs=[
            pl.BlockSpec((LANES, tq), lambda b, p, i: (p, b * nq + i)),
            pl.BlockSpec((seq, LANES), lambda b, p, i: (b, p)),
            pl.BlockSpec((nb, LANES, blk), lambda b, p, i: (b, p, 0)),
        ],
        out_specs=pl.BlockSpec((tq, LANES), lambda b, p, i: (b * nq + i, p)),
        out_shape=jax.ShapeDtypeStruct((batch * seq, d), BF16),
        scratch_shapes=[
            pltpu.VMEM((nb, LANES), F32),
            pltpu.VMEM((2, nb, tq), F32),
            pltpu.VMEM((2, tq), F32),
            pltpu.VMEM((2, A_HEAD_DIM + SUM_ROWS, tq), F32),
            pltpu.VMEM((2, 2 * (tq // blk), blk, blk), F32),
        ],
        compiler_params=_params(("parallel", "parallel", "arbitrary")),
        name="moba",
    )(qt, z, vt)


def _hgrn_kernel(q_ref, f_ref, i_ref, og_ref, lbl_ref, ng_ref, o_ref, st_ref, *, lt, group):
    c_len, sub = HGRN_CHUNK, HGRN_SUB
    n_sub = c_len // sub
    dk = B_KEY_DIM
    heads = [slice(g * LANES, (g + 1) * LANES) for g in range(group)]

    @pl.when(pl.program_id(2) == 0)
    def _():
        st_ref[...] = jnp.zeros(st_ref.shape, F32)

    lg = lbl_ref[...]
    e = jnp.exp(lg - jnp.max(lg, axis=0, keepdims=True))
    lb = e[0:1, :] / jnp.sum(e, axis=0, keepdims=True)
    ng = ng_ref[...]

    tri = (lax.broadcasted_iota(jnp.int32, (c_len, c_len), 0)
           >= lax.broadcasted_iota(jnp.int32, (c_len, c_len), 1)).astype(BF16)
    width = group * dk
    half = sub // 2
    n_half = c_len // half
    rowi = lax.broadcasted_iota(jnp.int32, (n_half, half, width), 1)
    same_sub = (lax.broadcasted_iota(jnp.int32, (n_sub * half, n_sub * half), 0) // half
                == lax.broadcasted_iota(jnp.int32, (n_sub * half, n_sub * half), 1) // half)

    def step(c, states):
        rows = pl.ds(pl.multiple_of(c * c_len, c_len), c_len)
        qb = q_ref[rows, :].astype(F32)
        fb = f_ref[rows, :].astype(F32)
        v = i_ref[rows, :].astype(F32)
        og = og_ref[rows, :].astype(F32)

        q = qb * _sigmoid(qb) * (dk ** -0.5)
        fg = lb + (1.0 - lb) * _sigmoid(fb)
        logf = jnp.log2(fg)
        k = 1.0 - fg

        hi = logf.astype(BF16)
        r1 = logf - hi.astype(F32)
        mid = r1.astype(BF16)
        lo = (r1 - mid.astype(F32)).astype(BF16)
        b = _dot(tri, hi) + _dot(tri, mid) + _dot(tri, lo)

        qe = (q * jnp.exp2(b)).astype(BF16)
        o_inter = [_dot_nt(qe[:, hs], states[g].astype(BF16)) for g, hs in enumerate(heads)]

        v16 = v.astype(BF16)
        pair_a = []
        for s_i in range(1, n_sub):
            lo_r = s_i * sub
            ref_b = b[lo_r - 1:lo_r, :]
            qt = (q[lo_r:lo_r + sub, :] * jnp.exp2(b[lo_r:lo_r + sub, :] - ref_b)).astype(BF16)
            kt = (k[:lo_r, :] * jnp.exp2(ref_b - b[:lo_r, :])).astype(BF16)
            pair_a.append([_dot_nt(qt[:, hs], kt[:, hs]).astype(BF16) for hs in heads])
        pair_o = [[_dot(pair_a[s_i - 1][g], v16[:s_i * sub, hs]) for s_i in range(1, n_sub)]
                  for g, hs in enumerate(heads)]

        b3 = b.reshape(n_sub, sub, width)
        ref_h = b3[:, half - 1:half, :]
        q_h = q.reshape(n_sub, sub, width)[:, half:, :] * jnp.exp2(b3[:, half:, :] - ref_h)
        k_h = k.reshape(n_sub, sub, width)[:, :half, :] * jnp.exp2(ref_h - b3[:, :half, :])
        q_h = q_h.reshape(n_sub * half, width).astype(BF16)
        k_h = k_h.reshape(n_sub * half, width).astype(BF16)
        v_h = v.reshape(n_sub, sub, width)[:, :half, :].reshape(n_sub * half, width).astype(BF16)
        half_a = [jnp.where(same_sub, _dot_nt(q_h[:, hs], k_h[:, hs]), 0.0).astype(BF16) for hs in heads]
        half_o = [_dot(half_a[g], v_h[:, hs]) for g, hs in enumerate(heads)]

        b_last = b[c_len - 1:c_len, :]
        kl = (k * jnp.exp2(b_last - b)).astype(BF16)
        decay = jnp.exp2(b_last)
        new_states = [states[g] * decay[:, hs] + _dot(v[:, hs].T.astype(BF16), kl[:, hs])
                      for g, hs in enumerate(heads)]

        b8 = b.reshape(n_half, half, width)
        q8 = q.reshape(n_half, half, width)
        k8 = k.reshape(n_half, half, width)
        v8 = v.reshape(n_half, half, width)
        od = [jnp.zeros((n_half, half, dk), F32) for _ in heads]
        for s_o in range(half):
            dec = jnp.exp2(jnp.where(rowi >= s_o, b8 - b8[:, s_o:s_o + 1, :], NEG))
            gq = q8 * dec * k8[:, s_o:s_o + 1, :]
            for g, hs in enumerate(heads):
                od[g] = od[g] + jnp.sum(gq[:, :, hs], axis=-1, keepdims=True) * v8[:, s_o:s_o + 1, hs]

        gate = og * _sigmoid(og)
        for g, hs in enumerate(heads):
            o = o_inter[g] + jnp.concatenate([jnp.zeros((sub, dk), F32)] + pair_o[g], axis=0)
            second = jnp.concatenate([jnp.zeros((n_sub, half, dk), F32), half_o[g].reshape(n_sub, half, dk)], axis=1)
            o = o + second.reshape(c_len, dk) + od[g].reshape(c_len, dk)
            o = o * lax.rsqrt(jnp.mean(o * o, axis=-1, keepdims=True) + EPS) * ng[:, hs]
            o_ref[rows, hs] = (o * gate[:, hs]).astype(BF16)
        return new_states

    def trip(n, carry):
        states = [st_ref[g] for g in range(group)]
        for u in range(HGRN_UNROLL):
            states = step(n * HGRN_UNROLL + u, states)
        for g in range(group):
            st_ref[g] = states[g]
        return carry

    lax.fori_loop(0, lt // (c_len * HGRN_UNROLL), trip, 0)


def _hgrn(z, lb_logits, norm_g, batch, seq, lt, group):
    nt = seq // lt
    width = group * LANES
    d = B_HEADS * LANES
    seg = lambda off: pl.BlockSpec((lt, width), lambda b, hh, t: (b * nt + t, off * (d // width) + hh))
    return pl.pallas_call(
        functools.partial(_hgrn_kernel, lt=lt, group=group),
        grid=(batch, B_HEADS // group, nt),
        in_specs=[
            seg(1), seg(2), seg(3), seg(4),
            pl.BlockSpec((lb_logits.shape[0], width), lambda b, hh, t: (0, hh)),
            pl.BlockSpec((1, width), lambda b, hh, t: (0, hh)),
        ],
        out_specs=pl.BlockSpec((lt, width), lambda b, hh, t: (b * nt + t, hh)),
        out_shape=jax.ShapeDtypeStruct((batch * seq, d), BF16),
        scratch_shapes=[pltpu.VMEM((group, LANES, B_KEY_DIM), F32)],
        compiler_params=_params(("parallel", "parallel", "arbitrary")),
        name="hgrn",
    )(z, z, z, z, lb_logits, norm_g)


def _merge_kernel(x_ref, a_ref, hg_ref, ga_ref, gb_ref, wa_ref, wb_ref, wo_ref, gx_ref, wq_ref,
                  km_ref, vm_ref, wox_ref, o_ref):
    ya = _dot(a_ref[...], wa_ref[...])
    yb = _dot(hg_ref[...], wb_ref[...])
    mg = _sigmoid(ga_ref[...].astype(F32)) * ya + _sigmoid(gb_ref[...].astype(F32)) * yb
    h = x_ref[...] + _dot(mg.astype(BF16), wo_ref[...])

    hn = _rms(h, gx_ref[...]).astype(BF16)
    q = _dot(hn, wq_ref[...])
    d = q.shape[1]
    xd = d // X_HEADS
    outs = []
    for hh in range(X_HEADS):
        sl = slice(hh * xd, (hh + 1) * xd)
        s = _dot_nt(q[:, sl].astype(BF16), km_ref[:, sl]) * (xd ** -0.5)
        p = jnp.exp(s - jnp.max(s, axis=-1, keepdims=True))
        oh = _dot(p.astype(BF16), vm_ref[:, sl])
        outs.append(oh * (1.0 / jnp.sum(p, axis=-1, keepdims=True)))
    o = jnp.concatenate(outs, axis=-1)
    o_ref[...] = h + _dot(o.astype(BF16), wox_ref[...])


def _merge(x2d, attn, hg, z, wa, wb, wo, gx, wq, kvm, wox, batch, seq, n_mem, tm):
    d = x2d.shape[1]
    nt = seq // tm
    gcol = (z.shape[1] - 2 * d) // d
    row = lambda b, t: (b * nt + t, 0)
    full = lambda b, t: (0, 0)
    wspec = pl.BlockSpec((d, d), full)
    return pl.pallas_call(
        _merge_kernel,
        grid=(batch, nt),
        in_specs=[
            pl.BlockSpec((tm, d), row),
            pl.BlockSpec((tm, d), row),
            pl.BlockSpec((tm, d), row),
            pl.BlockSpec((tm, d), lambda b, t: (b * nt + t, gcol)),
            pl.BlockSpec((tm, d), lambda b, t: (b * nt + t, gcol + 1)),
            wspec, wspec, wspec,
            pl.BlockSpec((1, d), full),
            wspec,
            pl.BlockSpec((n_mem, d), lambda b, t: (b, 0)),
            pl.BlockSpec((n_mem, d), lambda b, t: (b, 1)),
            wspec,
        ],
        out_specs=pl.BlockSpec((tm, d), row),
        out_shape=jax.ShapeDtypeStruct(x2d.shape, F32),
        compiler_params=_params(("parallel", "parallel")),
        name="merge",
    )(x2d, attn, hg, z, z, wa, wb, wo, gx, wq, kvm, kvm, wox)


def _ffn_kernel(h_ref, g_ref, wi_ref, wo_ref, gf_ref, o_ref, *, hidden, bounds):
    h = h_ref[...]
    hn = _rms(h, g_ref[...]).astype(BF16)
    acc = h
    for lo, hi in bounds:
        gate = _dot(hn, wi_ref[:, lo:hi])
        up = _dot(hn, wi_ref[:, hidden + lo:hidden + hi])
        act = (gate * _sigmoid(gate) * up).astype(BF16)
        acc = acc + _dot(act, wo_ref[lo:hi, :])
    o_ref[...] = _rms(acc, gf_ref[...])


def _ffn(h2d, g, wi, wo, gf, tm):
    t, d = h2d.shape
    hidden = wo.shape[0]
    step = 1024
    bounds = tuple((lo, min(lo + step, hidden)) for lo in range(0, hidden, step))
    full = lambda i: (0, 0)
    return pl.pallas_call(
        functools.partial(_ffn_kernel, hidden=hidden, bounds=bounds),
        grid=(t // tm,),
        in_specs=[
            pl.BlockSpec((tm, d), lambda i: (i, 0)),
            pl.BlockSpec((1, d), full),
            pl.BlockSpec(wi.shape, full, pipeline_mode=pl.Buffered(1)),
            pl.BlockSpec(wo.shape, full, pipeline_mode=pl.Buffered(1)),
            pl.BlockSpec((1, d), full),
        ],
        out_specs=pl.BlockSpec((tm, d), lambda i: (i, 0)),
        out_shape=jax.ShapeDtypeStruct(h2d.shape, F32),
        compiler_params=_params(("parallel",)),
        name="ffn",
    )(h2d, g, wi, wo, gf)


def _rope_tables(positions):
    half = A_HEAD_DIM // 2
    inv_freq = jnp.power(ROPE_THETA, -jnp.arange(0, A_HEAD_DIM, 2, dtype=F32) / A_HEAD_DIM)
    pos = positions.astype(F32).reshape(-1)
    lane = jnp.arange(LANES)
    ang = pos[:, None] * inv_freq[lane % half][None, :]
    sign = jnp.where((lane % A_HEAD_DIM) < half, -1.0, 1.0).astype(F32)
    ang_t = inv_freq[:, None] * pos[None, :]
    return jnp.cos(ang), jnp.sin(ang) * sign, jnp.cos(ang_t), jnp.sin(ang_t)


def kernel(x, mem, positions, norm_mix_g, w_in, hgrn_lb_logits, hgrn_norm_g, w_br_attn, w_br_hgrn, w_out,
           norm_x_g, norm_mem_g, wq_x, wkv_x, wo_x, norm_ffn_g, w_ffn_in, w_ffn_out, final_norm_g):
    batch, seq, d = x.shape
    n_mem = mem.shape[1]
    depth = w_in.shape[0]
    assert depth == 1 and d == A_HEADS * A_HEAD_DIM == B_HEADS * LANES and B_KEY_DIM == LANES
    tm = min(512, seq)
    tq = MOBA_QUERY_BLOCKS * MOBA_BLOCK
    assert seq % tq == 0 and seq % tm == 0 and w_in.shape[2] == 9 * d
    lt = min(1024, seq)
    x2d = x.reshape(batch * seq, d)
    cos_t, sin_t, cos_tt, sin_tt = _rope_tables(positions)
    row = lambda g: g.reshape(1, -1).astype(F32)
    bf = lambda w: w.astype(BF16)

    w_in16 = bf(w_in[0])
    wt_qv = jnp.stack([w_in16[:, :d].T, w_in16[:, 2 * d:3 * d].T])

    kvm = _memkv(mem.reshape(batch * n_mem, d), row(norm_mem_g[0]), bf(wkv_x[0]), n_mem)
    z, qt, vt = _inproj(x2d, row(norm_mix_g[0]), w_in16, wt_qv, cos_t, sin_t, cos_tt, sin_tt, tm)
    attn = _moba(z, qt, vt, batch, seq, tq)
    hg = _hgrn(z, hgrn_lb_logits.astype(F32), row(hgrn_norm_g[0]), batch, seq, lt, HGRN_GROUP)
    h = _merge(x2d, attn, hg, z, bf(w_br_attn[0]), bf(w_br_hgrn[0]), bf(w_out[0]), row(norm_x_g[0]),
               bf(wq_x[0]), kvm, bf(wo_x[0]), batch, seq, n_mem, tm)
    out = _ffn(h, row(norm_ffn_g[0]), bf(w_ffn_in[0]), bf(w_ffn_out[0]), row(final_norm_g), tm)
    return out.reshape(batch, seq, d)
```

```python
import functools

import jax
import jax.numpy as jnp
from jax import lax
from jax.experimental import pallas as pl
from jax.experimental.pallas import tpu as pltpu

F32 = jnp.float32
BF16 = jnp.bfloat16

EPS = 1e-6
NEG = -1e30
ROPE_THETA = 10000.0

A_HEADS = 16
A_HEAD_DIM = 64
MOBA_BLOCK = 256
MOBA_TOPK = 3
MOBA_QUERY_BLOCKS = 4
MOBA_LOOKAHEAD = 3
B_HEADS = 8
B_KEY_DIM = 128
X_HEADS = 4
HGRN_CHUNK = 64
HGRN_SUB = 16
HGRN_GROUP = 8
HGRN_UNROLL = 4

LANES = 128
SUM_ROWS = 16
LOG2_E = 1.4426950408889634
VMEM_LIMIT = 56 * 1024 * 1024


def _params(sem):
    return pltpu.CompilerParams(dimension_semantics=sem, vmem_limit_bytes=VMEM_LIMIT)


def _rms(x, g):
    return x * lax.rsqrt(jnp.mean(x * x, axis=-1, keepdims=True) + EPS) * g


def _dot(a, b):
    return jnp.dot(a, b, preferred_element_type=F32)


def _dot_nt(a, b):
    return lax.dot_general(a, b, (((1,), (1,)), ((), ())), preferred_element_type=F32)


def _sigmoid(x):
    return 1.0 / (1.0 + jnp.exp(-x))


def _memkv_kernel(mem_ref, g_ref, w_ref, o_ref):
    y = _rms(mem_ref[...], g_ref[...])
    o_ref[...] = _dot(y.astype(BF16), w_ref[...]).astype(BF16)


def _memkv(mem2d, g, w, n_mem):
    rows, d = mem2d.shape
    return pl.pallas_call(
        _memkv_kernel,
        grid=(rows // n_mem,),
        in_specs=[
            pl.BlockSpec((n_mem, d), lambda b: (b, 0)),
            pl.BlockSpec((1, d), lambda b: (0, 0)),
            pl.BlockSpec(w.shape, lambda b: (0, 0)),
        ],
        out_specs=pl.BlockSpec((n_mem, w.shape[1]), lambda b: (b, 0)),
        out_shape=jax.ShapeDtypeStruct((rows, w.shape[1]), BF16),
        compiler_params=_params(("parallel",)),
        name="memkv",
    )(mem2d, g, w)


def _inproj_kernel(x_ref, g_ref, w_ref, wt_ref, cos_ref, sin_ref, cost_ref, sint_ref, z_ref, qt_ref, vt_ref, *, tm):
    hd, half = A_HEAD_DIM, A_HEAD_DIM // 2
    d = x_ref.shape[1]
    xn = _rms(x_ref[...], g_ref[...]).astype(BF16)

    acc = _dot_nt(wt_ref[0], xn)
    cos = cost_ref[...]
    sin = sint_ref[...]
    scale = hd ** -0.5 * LOG2_E
    for h in range(d // hd):
        t1 = acc[h * hd:h * hd + half, :]
        t2 = acc[h * hd + half:(h + 1) * hd, :]
        qt_ref[h * hd:h * hd + half, :] = ((t1 * cos - t2 * sin) * scale).astype(BF16)
        qt_ref[h * hd + half:(h + 1) * hd, :] = ((t2 * cos + t1 * sin) * scale).astype(BF16)

    acc = _dot(xn, w_ref[:, d:2 * d])
    cos = cos_ref[...]
    sin = sin_ref[...]
    lane = lax.broadcasted_iota(jnp.int32, cos.shape, 1)
    low = (lane % hd) < half
    for c in range(d // LANES):
        t = acc[:, c * LANES:(c + 1) * LANES]
        rot = jnp.where(low, pltpu.roll(t, LANES - half, 1), pltpu.roll(t, half, 1))
        z_ref[:, c * LANES:(c + 1) * LANES] = (t * cos + rot * sin).astype(BF16)

    acc = _dot_nt(wt_ref[1], xn).astype(BF16)
    for n in range(tm // MOBA_BLOCK):
        vt_ref[n] = acc[:, n * MOBA_BLOCK:(n + 1) * MOBA_BLOCK]

    for s in range(3, w_ref.shape[1] // d):
        z_ref[:, (s - 2) * d:(s - 1) * d] = _dot(xn, w_ref[:, s * d:(s + 1) * d]).astype(BF16)


def _inproj(x2d, g, w, wt, cos_t, sin_t, cos_tt, sin_tt, tm):
    t, d = x2d.shape
    blk = MOBA_BLOCK
    full = lambda i: (0, 0)
    return pl.pallas_call(
        functools.partial(_inproj_kernel, tm=tm),
        grid=(t // tm,),
        in_specs=[
            pl.BlockSpec((tm, d), lambda i: (i, 0)),
            pl.BlockSpec((1, d), full),
            pl.BlockSpec(w.shape, full, pipeline_mode=pl.Buffered(1)),
            pl.BlockSpec(wt.shape, lambda i: (0, 0, 0), pipeline_mode=pl.Buffered(1)),
            pl.BlockSpec((tm, LANES), lambda i: (i, 0)),
            pl.BlockSpec((tm, LANES), lambda i: (i, 0)),
            pl.BlockSpec((A_HEAD_DIM // 2, tm), lambda i: (0, i)),
            pl.BlockSpec((A_HEAD_DIM // 2, tm), lambda i: (0, i)),
        ],
        out_specs=[
            pl.BlockSpec((tm, w.shape[1] - 2 * d), lambda i: (i, 0)),
            pl.BlockSpec((d, tm), lambda i: (0, i)),
            pl.BlockSpec((tm // blk, d, blk), lambda i: (i, 0, 0)),
        ],
        out_shape=[
            jax.ShapeDtypeStruct((t, w.shape[1] - 2 * d), BF16),
            jax.ShapeDtypeStruct((d, t), BF16),
            jax.ShapeDtypeStruct((t // blk, d, blk), BF16),
        ],
        compiler_params=_params(("parallel",)),
        name="inproj",
    )(x2d, g, w, wt, cos_t, sin_t, cos_tt, sin_tt)


def _moba_kernel(q_ref, k_ref, v_ref, o_ref, km_ref, sel_ref, m_ref, acc_ref, s_ref, *, nb, blk, tq):
    i = pl.program_id(2)
    hd = A_HEAD_DIM
    halves = tq // blk

    @pl.when(i == 0)
    def _prep():
        for n in range(nb):
            km_ref[n:n + 1, :] = jnp.mean(k_ref[n * blk:(n + 1) * blk, :].astype(F32), axis=0, keepdims=True)

    q_t = q_ref[...]
    row = lax.broadcasted_iota(jnp.int32, q_t.shape, 0)
    zero = jnp.zeros_like(q_t)
    qz = [jnp.where(row < hd, q_t, zero), jnp.where(row >= hd, q_t, zero)]

    km = km_ref[...]
    km_hi = km.astype(BF16)
    km_lo = (km - km_hi.astype(F32)).astype(BF16)
    blk_id = lax.broadcasted_iota(jnp.int32, (nb, tq), 0)
    q_blk = i * halves + lax.broadcasted_iota(jnp.int32, (nb, tq), 1) // blk
    gates = [_dot(km_hi, qz[h]) + _dot(km_lo, qz[h]) for h in range(2)]

    def select(h):
        def run():
            g = jnp.where(blk_id < q_blk, gates[h], -jnp.inf)
            sel = jnp.zeros((nb, tq), F32)
            for _ in range(MOBA_TOPK):
                mx = jnp.max(g, axis=0, keepdims=True)
                first = jnp.min(jnp.where(g == mx, blk_id, nb), axis=0, keepdims=True)
                first = jnp.where(mx > -jnp.inf, first, -1)
                pick = blk_id == first
                sel = jnp.where(pick, 1.0, sel)
                g = jnp.where(pick, -jnp.inf, g)
            sel_ref[h] = sel
        return run

    m_ref[...] = jnp.full(m_ref.shape, NEG, F32)
    acc_ref[...] = jnp.zeros(acc_ref.shape, F32)
    ones = jnp.ones((SUM_ROWS, blk), BF16)

    groups = [(h, c) for h in range(2) for c in range(halves)]

    def score(slot, j, g):
        def run():
            h, c = groups[g]
            kj = k_ref[pl.ds(pl.multiple_of(j * blk, blk), blk), :]
            s_ref[slot, g] = _dot(kj, qz[h][:, c * blk:(c + 1) * blk])
        return run

    def finish(slot, j, g, own):
        def run():
            h, c = groups[g]
            cs = slice(c * blk, (c + 1) * blk)
            s = s_ref[slot, g]
            m_old = m_ref[h:h + 1, cs]
            if own:
                s = jnp.where(causal, s, NEG)
                m_new = jnp.maximum(m_old, jnp.max(s, axis=0, keepdims=True))
                m_use = m_new
            else:
                on = sel_ref[h, pl.ds(j, 1), cs] > 0.0
                m_new = jnp.where(on, jnp.maximum(m_old, jnp.max(s, axis=0, keepdims=True)), m_old)
                m_use = jnp.where(on, m_new, -NEG)
            p = jnp.exp2(s - m_use).astype(BF16)
            pv = _dot(jnp.concatenate([v_ref[j, h * hd:(h + 1) * hd, :], ones], axis=0), p)
            acc_ref[h, :, cs] = jnp.exp2(m_old - m_new) * acc_ref[h, :, cs] + pv
            m_ref[h:h + 1, cs] = m_new
        return run

    def overlap(scores, finishes):
        for run in scores[:MOBA_LOOKAHEAD]:
            run()
        rest = scores[MOBA_LOOKAHEAD:]
        for run in finishes:
            run()
            if rest:
                rest.pop(0)()
        for run in rest:
            run()

    everyone = range(len(groups))
    overlap([score(0, 0, g) for g in everyone], [select(0), select(1)])

    def body(jj, carry):
        j0 = halves * jj
        for t in range(halves):
            overlap([score((t + 1) % 2, j0 + t + 1, g) for g in everyone],
                    [finish(t % 2, j0 + t, g, False) for g in everyone])
        return carry

    lax.fori_loop(0, i, body, 0)

    causal = (lax.broadcasted_iota(jnp.int32, (blk, blk), 0)
              <= lax.broadcasted_iota(jnp.int32, (blk, blk), 1))
    base = i * halves
    for t in range(halves):
        later = [g for g in everyone if groups[g][1] > t]
        overlap([score((t + 1) % 2, base + t + 1, g) for g in later],
                [finish(t % 2, base + t, g, groups[g][1] == t) for g in everyone if groups[g][1] >= t])

    out_t = jnp.concatenate([acc_ref[h, :hd, :] * (1.0 / acc_ref[h, hd:hd + 1, :]) for h in range(2)], axis=0)
    o_ref[...] = out_t.T.astype(BF16)


def _moba(z, qt, vt, batch, seq, tq):
    blk = MOBA_BLOCK
    nb = seq // blk
    nq = seq // tq
    d = qt.shape[0]
    pairs = d // LANES
    return pl.pallas_call(
        functools.partial(_moba_kernel, nb=nb, blk=blk, tq=tq),
        grid=(batch, pairs, nq),
        in_specs=[
            pl.BlockSpec((LANES, tq), lambda b, p, i: (p, b * nq + i)),
            pl.BlockSpec((seq, LANES), lambda b, p, i: (b, p)),
            pl.BlockSpec((nb, LANES, blk), lambda b, p, i: (b, p, 0)),
        ],
        out_specs=pl.BlockSpec((tq, LANES), lambda b, p, i: (b * nq + i, p)),
        out_shape=jax.ShapeDtypeStruct((batch * seq, d), BF16),
        scratch_shapes=[
            pltpu.VMEM((nb, LANES), F32),
            pltpu.VMEM((2, nb, tq), F32),
            pltpu.VMEM((2, tq), F32),
            pltpu.VMEM((2, A_HEAD_DIM + SUM_ROWS, tq), F32),
            pltpu.VMEM((2, 2 * (tq // blk), blk, blk), F32),
        ],
        compiler_params=_params(("parallel", "parallel", "arbitrary")),
        name="moba",
    )(qt, z, vt)


def _hgrn_kernel(q_ref, f_ref, i_ref, og_ref, lbl_ref, ng_ref, o_ref, st_ref, *, lt, group):
    c_len, sub = HGRN_CHUNK, HGRN_SUB
    n_sub = c_len // sub
    dk = B_KEY_DIM
    heads = [slice(g * LANES, (g + 1) * LANES) for g in range(group)]

    @pl.when(pl.program_id(2) == 0)
    def _():
        st_ref[...] = jnp.zeros(st_ref.shape, F32)

    lg = lbl_ref[...]
    e = jnp.exp(lg - jnp.max(lg, axis=0, keepdims=True))
    lb = e[0:1, :] / jnp.sum(e, axis=0, keepdims=True)
    ng = ng_ref[...]

    tri = (lax.broadcasted_iota(jnp.int32, (c_len, c_len), 0)
           >= lax.broadcasted_iota(jnp.int32, (c_len, c_len), 1)).astype(BF16)
    width = group * dk
    half = sub // 2
    n_half = c_len // half
    rowi = lax.broadcasted_iota(jnp.int32, (n_half, half, width), 1)
    same_sub = (lax.broadcasted_iota(jnp.int32, (n_sub * half, n_sub * half), 0) // half
                == lax.broadcasted_iota(jnp.int32, (n_sub * half, n_sub * half), 1) // half)

    def step(c, states):
        rows = pl.ds(pl.multiple_of(c * c_len, c_len), c_len)
        qb = q_ref[rows, :].astype(F32)
        fb = f_ref[rows, :].astype(F32)
        v = i_ref[rows, :].astype(F32)
        og = og_ref[rows, :].astype(F32)

        q = qb * _sigmoid(qb) * (dk ** -0.5)
        fg = lb + (1.0 - lb) * _sigmoid(fb)
        logf = jnp.log2(fg)
        k = 1.0 - fg

        hi = logf.astype(BF16)
        r1 = logf - hi.astype(F32)
        mid = r1.astype(BF16)
        lo = (r1 - mid.astype(F32)).astype(BF16)
        b = _dot(tri, hi) + _dot(tri, mid) + _dot(tri, lo)

        qe = (q * jnp.exp2(b)).astype(BF16)
        o_inter = [_dot_nt(qe[:, hs], states[g].astype(BF16)) for g, hs in enumerate(heads)]

        v16 = v.astype(BF16)
        pair_a = []
        for s_i in range(1, n_sub):
            lo_r = s_i * sub
            ref_b = b[lo_r - 1:lo_r, :]
            qt = (q[lo_r:lo_r + sub, :] * jnp.exp2(b[lo_r:lo_r + sub, :] - ref_b)).astype(BF16)
            kt = (k[:lo_r, :] * jnp.exp2(ref_b - b[:lo_r, :])).astype(BF16)
            pair_a.append([_dot_nt(qt[:, hs], kt[:, hs]).astype(BF16) for hs in heads])
        pair_o = [[_dot(pair_a[s_i - 1][g], v16[:s_i * sub, hs]) for s_i in range(1, n_sub)]
                  for g, hs in enumerate(heads)]

        b3 = b.reshape(n_sub, sub, width)
        ref_h = b3[:, half - 1:half, :]
        q_h = q.reshape(n_sub, sub, width)[:, half:, :] * jnp.exp2(b3[:, half:, :] - ref_h)
        k_h = k.reshape(n_sub, sub, width)[:, :half, :] * jnp.exp2(ref_h - b3[:, :half, :])
        q_h = q_h.reshape(n_sub * half, width).astype(BF16)
        k_h = k_h.reshape(n_sub * half, width).astype(BF16)
        v_h = v.reshape(n_sub, sub, width)[:, :half, :].reshape(n_sub * half, width).astype(BF16)
        half_a = [jnp.where(same_sub, _dot_nt(q_h[:, hs], k_h[:, hs]), 0.0).astype(BF16) for hs in heads]
        half_o = [_dot(half_a[g], v_h[:, hs]) for g, hs in enumerate(heads)]

        b_last = b[c_len - 1:c_len, :]
        kl = (k * jnp.exp2(b_last - b)).astype(BF16)
        decay = jnp.exp2(b_last)
        new_states = [states[g] * decay[:, hs] + _dot(v[:, hs].T.astype(BF16), kl[:, hs])
                      for g, hs in enumerate(heads)]

        b8 = b.reshape(n_half, half, width)
        q8 = q.reshape(n_half, half, width)
        k8 = k.reshape(n_half, half, width)
        v8 = v.reshape(n_half, half, width)
        od = [jnp.zeros((n_half, half, dk), F32) for _ in heads]
        for s_o in range(half):
            dec = jnp.exp2(jnp.where(rowi >= s_o, b8 - b8[:, s_o:s_o + 1, :], NEG))
            gq = q8 * dec * k8[:, s_o:s_o + 1, :]
            for g, hs in enumerate(heads):
                od[g] = od[g] + jnp.sum(gq[:, :, hs], axis=-1, keepdims=True) * v8[:, s_o:s_o + 1, hs]

        gate = og * _sigmoid(og)
        for g, hs in enumerate(heads):
            o = o_inter[g] + jnp.concatenate([jnp.zeros((sub, dk), F32)] + pair_o[g], axis=0)
            second = jnp.concatenate([jnp.zeros((n_sub, half, dk), F32), half_o[g].reshape(n_sub, half, dk)], axis=1)
            o = o + second.reshape(c_len, dk) + od[g].reshape(c_len, dk)
            o = o * lax.rsqrt(jnp.mean(o * o, axis=-1, keepdims=True) + EPS) * ng[:, hs]
            o_ref[rows, hs] = (o * gate[:, hs]).astype(BF16)
        return new_states

    def trip(n, carry):
        states = [st_ref[g] for g in range(group)]
        for u in range(HGRN_UNROLL):
            states = step(n * HGRN_UNROLL + u, states)
        for g in range(group):
            st_ref[g] = states[g]
        return carry

    lax.fori_loop(0, lt // (c_len * HGRN_UNROLL), trip, 0)


def _hgrn(z, lb_logits, norm_g, batch, seq, lt, group):
    nt = seq // lt
    width = group * LANES
    d = B_HEADS * LANES
    seg = lambda off: pl.BlockSpec((lt, width), lambda b, hh, t: (b * nt + t, off * (d // width) + hh))
    return pl.pallas_call(
        functools.partial(_hgrn_kernel, lt=lt, group=group),
        grid=(batch, B_HEADS // group, nt),
        in_specs=[
            seg(1), seg(2), seg(3), seg(4),
            pl.BlockSpec((lb_logits.shape[0], width), lambda b, hh, t: (0, hh)),
            pl.BlockSpec((1, width), lambda b, hh, t: (0, hh)),
        ],
        out_specs=pl.BlockSpec((lt, width), lambda b, hh, t: (b * nt + t, hh)),
        out_shape=jax.ShapeDtypeStruct((batch * seq, d), BF16),
        scratch_shapes=[pltpu.VMEM((group, LANES, B_KEY_DIM), F32)],
        compiler_params=_params(("parallel", "parallel", "arbitrary")),
        name="hgrn",
    )(z, z, z, z, lb_logits, norm_g)


def _merge_kernel(x_ref, a_ref, hg_ref, ga_ref, gb_ref, wa_ref, wb_ref, wo_ref, gx_ref, wq_ref,
                  km_ref, vm_ref, wox_ref, o_ref):
    ya = _dot(a_ref[...], wa_ref[...])
    yb = _dot(hg_ref[...], wb_ref[...])
    mg = _sigmoid(ga_ref[...].astype(F32)) * ya + _sigmoid(gb_ref[...].astype(F32)) * yb
    h = x_ref[...] + _dot(mg.astype(BF16), wo_ref[...])

    hn = _rms(h, gx_ref[...]).astype(BF16)
    q = _dot(hn, wq_ref[...])
    d = q.shape[1]
    xd = d // X_HEADS
    outs = []
    for hh in range(X_HEADS):
        sl = slice(hh * xd, (hh + 1) * xd)
        s = _dot_nt(q[:, sl].astype(BF16), km_ref[:, sl]) * (xd ** -0.5)
        p = jnp.exp(s - jnp.max(s, axis=-1, keepdims=True))
        oh = _dot(p.astype(BF16), vm_ref[:, sl])
        outs.append(oh * (1.0 / jnp.sum(p, axis=-1, keepdims=True)))
    o = jnp.concatenate(outs, axis=-1)
    o_ref[...] = h + _dot(o.astype(BF16), wox_ref[...])


def _merge(x2d, attn, hg, z, wa, wb, wo, gx, wq, kvm, wox, batch, seq, n_mem, tm):
    d = x2d.shape[1]
    nt = seq // tm
    gcol = (z.shape[1] - 2 * d) // d
    row = lambda b, t: (b * nt + t, 0)
    full = lambda b, t: (0, 0)
    wspec = pl.BlockSpec((d, d), full)
    return pl.pallas_call(
        _merge_kernel,
        grid=(batch, nt),
        in_specs=[
            pl.BlockSpec((tm, d), row),
            pl.BlockSpec((tm, d), row),
            pl.BlockSpec((tm, d), row),
            pl.BlockSpec((tm, d), lambda b, t: (b * nt + t, gcol)),
            pl.BlockSpec((tm, d), lambda b, t: (b * nt + t, gcol + 1)),
            wspec, wspec, wspec,
            pl.BlockSpec((1, d), full),
            wspec,
            pl.BlockSpec((n_mem, d), lambda b, t: (b, 0)),
            pl.BlockSpec((n_mem, d), lambda b, t: (b, 1)),
            wspec,
        ],
        out_specs=pl.BlockSpec((tm, d), row),
        out_shape=jax.ShapeDtypeStruct(x2d.shape, F32),
        compiler_params=_params(("parallel", "parallel")),
        name="merge",
    )(x2d, attn, hg, z, z, wa, wb, wo, gx, wq, kvm, kvm, wox)


def _ffn_kernel(h_ref, g_ref, wi_ref, wo_ref, gf_ref, o_ref, *, hidden, bounds):
    h = h_ref[...]
    hn = _rms(h, g_ref[...]).astype(BF16)
    acc = h
    for lo, hi in bounds:
        gate = _dot(hn, wi_ref[:, lo:hi])
        up = _dot(hn, wi_ref[:, hidden + lo:hidden + hi])
        act = (gate * _sigmoid(gate) * up).astype(BF16)
        acc = acc + _dot(act, wo_ref[lo:hi, :])
    o_ref[...] = _rms(acc, gf_ref[...])


def _ffn(h2d, g, wi, wo, gf, tm):
    t, d = h2d.shape
    hidden = wo.shape[0]
    step = 1024
    bounds = tuple((lo, min(lo + step, hidden)) for lo in range(0, hidden, step))
    full = lambda i: (0, 0)
    return pl.pallas_call(
        functools.partial(_ffn_kernel, hidden=hidden, bounds=bounds),
        grid=(t // tm,),
        in_specs=[
            pl.BlockSpec((tm, d), lambda i: (i, 0)),
            pl.BlockSpec((1, d), full),
            pl.BlockSpec(wi.shape, full, pipeline_mode=pl.Buffered(1)),
            pl.BlockSpec(wo.shape, full, pipeline_mode=pl.Buffered(1)),
            pl.BlockSpec((1, d), full),
        ],
        out_specs=pl.BlockSpec((tm, d), lambda i: (i, 0)),
        out_shape=jax.ShapeDtypeStruct(h2d.shape, F32),
        compiler_params=_params(("parallel",)),
        name="ffn",
    )(h2d, g, wi, wo, gf)


def _rope_tables(positions):
    half = A_HEAD_DIM // 2
    inv_freq = jnp.power(ROPE_THETA, -jnp.arange(0, A_HEAD_DIM, 2, dtype=F32) / A_HEAD_DIM)
    pos = positions.astype(F32).reshape(-1)
    lane = jnp.arange(LANES)
    ang = pos[:, None] * inv_freq[lane % half][None, :]
    sign = jnp.where((lane % A_HEAD_DIM) < half, -1.0, 1.0).astype(F32)
    ang_t = inv_freq[:, None] * pos[None, :]
    return jnp.cos(ang), jnp.sin(ang) * sign, jnp.cos(ang_t), jnp.sin(ang_t)


def kernel(x, mem, positions, norm_mix_g, w_in, hgrn_lb_logits, hgrn_norm_g, w_br_attn, w_br_hgrn, w_out,
           norm_x_g, norm_mem_g, wq_x, wkv_x, wo_x, norm_ffn_g, w_ffn_in, w_ffn_out, final_norm_g):
    batch, seq, d = x.shape
    n_mem = mem.shape[1]
    depth = w_in.shape[0]
    assert depth == 1 and d == A_HEADS * A_HEAD_DIM == B_HEADS * LANES and B_KEY_DIM == LANES
    tm = min(512, seq)
    tq = MOBA_QUERY_BLOCKS * MOBA_BLOCK
    assert seq % tq == 0 and seq % tm == 0 and w_in.shape[2] == 9 * d
    lt = min(1024, seq)
    x2d = x.reshape(batch * seq, d)
    cos_t, sin_t, cos_tt, sin_tt = _rope_tables(positions)
    row = lambda g: g.reshape(1, -1).astype(F32)
    bf = lambda w: w.astype(BF16)

    w_in16 = bf(w_in[0])
    wt_qv = jnp.stack([w_in16[:, :d].T, w_in16[:, 2 * d:3 * d].T])

    kvm = _memkv(mem.reshape(batch * n_mem, d), row(norm_mem_g[0]), bf(wkv_x[0]), n_mem)
    z, qt, vt = _inproj(x2d, row(norm_mix_g[0]), w_in16, wt_qv, cos_t, sin_t, cos_tt, sin_tt, tm)
    attn = _moba(z, qt, vt, batch, seq, tq)
    hg = _hgrn(z, hgrn_lb_logits.astype(F32), row(hgrn_norm_g[0]), batch, seq, lt, HGRN_GROUP)
    h = _merge(x2d, attn, hg, z, bf(w_br_attn[0]), bf(w_br_hgrn[0]), bf(w_out[0]), row(norm_x_g[0]),
               bf(wq_x[0]), kvm, bf(wo_x[0]), batch, seq, n_mem, tm)
    out = _ffn(h, row(norm_ffn_g[0]), bf(w_ffn_in[0]), bf(w_ffn_out[0]), row(final_norm_g), tm)
    return out.reshape(batch, seq, d)
```

```python
import functools

import jax
import jax.numpy as jnp
from jax import lax
from jax.experimental import pallas as pl
from jax.experimental.pallas import tpu as pltpu

F32 = jnp.float32
BF16 = jnp.bfloat16

EPS = 1e-6
NEG = -1e30
ROPE_THETA = 10000.0

A_HEADS = 16
A_HEAD_DIM = 64
MOBA_BLOCK = 256
MOBA_TOPK = 3
MOBA_QUERY_BLOCKS = 4
MOBA_LOOKAHEAD = 3
B_HEADS = 8
B_KEY_DIM = 128
X_HEADS = 4
HGRN_CHUNK = 64
HGRN_SUB = 16
HGRN_GROUP = 8
HGRN_UNROLL = 4

LANES = 128
SUM_ROWS = 16
LOG2_E = 1.4426950408889634
VMEM_LIMIT = 56 * 1024 * 1024


def _params(sem):
    return pltpu.CompilerParams(dimension_semantics=sem, vmem_limit_bytes=VMEM_LIMIT)


def _rms(x, g):
    return x * lax.rsqrt(jnp.mean(x * x, axis=-1, keepdims=True) + EPS) * g


def _dot(a, b):
    return jnp.dot(a, b, preferred_element_type=F32)


def _dot_nt(a, b):
    return lax.dot_general(a, b, (((1,), (1,)), ((), ())), preferred_element_type=F32)


def _sigmoid(x):
    return 1.0 / (1.0 + jnp.exp(-x))


def _memkv_kernel(mem_ref, g_ref, w_ref, o_ref):
    y = _rms(mem_ref[...], g_ref[...])
    o_ref[...] = _dot(y.astype(BF16), w_ref[...]).astype(BF16)


def _memkv(mem2d, g, w, n_mem):
    rows, d = mem2d.shape
    return pl.pallas_call(
        _memkv_kernel,
        grid=(rows // n_mem,),
        in_specs=[
            pl.BlockSpec((n_mem, d), lambda b: (b, 0)),
            pl.BlockSpec((1, d), lambda b: (0, 0)),
            pl.BlockSpec(w.shape, lambda b: (0, 0)),
        ],
        out_specs=pl.BlockSpec((n_mem, w.shape[1]), lambda b: (b, 0)),
        out_shape=jax.ShapeDtypeStruct((rows, w.shape[1]), BF16),
        compiler_params=_params(("parallel",)),
        name="memkv",
    )(mem2d, g, w)


def _inproj_kernel(x_ref, g_ref, w_ref, wt_ref, cos_ref, sin_ref, cost_ref, sint_ref, z_ref, qt_ref, vt_ref, *, tm):
    hd, half = A_HEAD_DIM, A_HEAD_DIM // 2
    d = x_ref.shape[1]
    xn = _rms(x_ref[...], g_ref[...]).astype(BF16)

    acc = _dot_nt(wt_ref[0], xn)
    cos = cost_ref[...]
    sin = sint_ref[...]
    scale = hd ** -0.5 * LOG2_E
    for h in range(d // hd):
        t1 = acc[h * hd:h * hd + half, :]
        t2 = acc[h * hd + half:(h + 1) * hd, :]
        qt_ref[h * hd:h * hd + half, :] = ((t1 * cos - t2 * sin) * scale).astype(BF16)
        qt_ref[h * hd + half:(h + 1) * hd, :] = ((t2 * cos + t1 * sin) * scale).astype(BF16)

    acc = _dot(xn, w_ref[:, d:2 * d])
    cos = cos_ref[...]
    sin = sin_ref[...]
    lane = lax.broadcasted_iota(jnp.int32, cos.shape, 1)
    low = (lane % hd) < half
    for c in range(d // LANES):
        t = acc[:, c * LANES:(c + 1) * LANES]
        rot = jnp.where(low, pltpu.roll(t, LANES - half, 1), pltpu.roll(t, half, 1))
        z_ref[:, c * LANES:(c + 1) * LANES] = (t * cos + rot * sin).astype(BF16)

    acc = _dot_nt(wt_ref[1], xn).astype(BF16)
    for n in range(tm // MOBA_BLOCK):
        vt_ref[n] = acc[:, n * MOBA_BLOCK:(n + 1) * MOBA_BLOCK]

    for s in range(3, w_ref.shape[1] // d):
        z_ref[:, (s - 2) * d:(s - 1) * d] = _dot(xn, w_ref[:, s * d:(s + 1) * d]).astype(BF16)


def _inproj(x2d, g, w, wt, cos_t, sin_t, cos_tt, sin_tt, tm):
    t, d = x2d.shape
    blk = MOBA_BLOCK
    full = lambda i: (0, 0)
    return pl.pallas_call(
        functools.partial(_inproj_kernel, tm=tm),
        grid=(t // tm,),
        in_specs=[
            pl.BlockSpec((tm, d), lambda i: (i, 0)),
            pl.BlockSpec((1, d), full),
            pl.BlockSpec(w.shape, full, pipeline_mode=pl.Buffered(1)),
            pl.BlockSpec(wt.shape, lambda i: (0, 0, 0), pipeline_mode=pl.Buffered(1)),
            pl.BlockSpec((tm, LANES), lambda i: (i, 0)),
            pl.BlockSpec((tm, LANES), lambda i: (i, 0)),
            pl.BlockSpec((A_HEAD_DIM // 2, tm), lambda i: (0, i)),
            pl.BlockSpec((A_HEAD_DIM // 2, tm), lambda i: (0, i)),
        ],
        out_specs=[
            pl.BlockSpec((tm, w.shape[1] - 2 * d), lambda i: (i, 0)),
            pl.BlockSpec((d, tm), lambda i: (0, i)),
            pl.BlockSpec((tm // blk, d, blk), lambda i: (i, 0, 0)),
        ],
        out_shape=[
            jax.ShapeDtypeStruct((t, w.shape[1] - 2 * d), BF16),
            jax.ShapeDtypeStruct((d, t), BF16),
            jax.ShapeDtypeStruct((t // blk, d, blk), BF16),
        ],
        compiler_params=_params(("parallel",)),
        name="inproj",
    )(x2d, g, w, wt, cos_t, sin_t, cos_tt, sin_tt)


def _moba_kernel(q_ref, k_ref, v_ref, o_ref, km_ref, sel_ref, m_ref, acc_ref, s_ref, *, nb, blk, tq):
    i = pl.program_id(2)
    hd = A_HEAD_DIM
    halves = tq // blk

    @pl.when(i == 0)
    def _prep():
        for n in range(nb):
            km_ref[n:n + 1, :] = jnp.mean(k_ref[n * blk:(n + 1) * blk, :].astype(F32), axis=0, keepdims=True)

    q_t = q_ref[...]
    row = lax.broadcasted_iota(jnp.int32, q_t.shape, 0)
    zero = jnp.zeros_like(q_t)
    qz = [jnp.where(row < hd, q_t, zero), jnp.where(row >= hd, q_t, zero)]

    km = km_ref[...]
    km_hi = km.astype(BF16)
    km_lo = (km - km_hi.astype(F32)).astype(BF16)
    blk_id = lax.broadcasted_iota(jnp.int32, (nb, tq), 0)
    q_blk = i * halves + lax.broadcasted_iota(jnp.int32, (nb, tq), 1) // blk
    gates = [_dot(km_hi, qz[h]) + _dot(km_lo, qz[h]) for h in range(2)]

    def select(h):
        def run():
            g = jnp.where(blk_id < q_blk, gates[h], -jnp.inf)
            sel = jnp.zeros((nb, tq), F32)
            for _ in range(MOBA_TOPK):
                mx = jnp.max(g, axis=0, keepdims=True)
                first = jnp.min(jnp.where(g == mx, blk_id, nb), axis=0, keepdims=True)
                first = jnp.where(mx > -jnp.inf, first, -1)
                pick = blk_id == first
                sel = jnp.where(pick, 1.0, sel)
                g = jnp.where(pick, -jnp.inf, g)
            sel_ref[h] = sel
        return run

    m_ref[...] = jnp.full(m_ref.shape, NEG, F32)
    acc_ref[...] = jnp.zeros(acc_ref.shape, F32)
    ones = jnp.ones((SUM_ROWS, blk), BF16)

    groups = [(h, c) for h in range(2) for c in range(halves)]

    def score(slot, j, g):
        def run():
            h, c = groups[g]
            kj = k_ref[pl.ds(pl.multiple_of(j * blk, blk), blk), :]
            s_ref[slot, g] = _dot(kj, qz[h][:, c * blk:(c + 1) * blk])
        return run

    def finish(slot, j, g, own):
        def run():
            h, c = groups[g]
            cs = slice(c * blk, (c + 1) * blk)
            s = s_ref[slot, g]
            m_old = m_ref[h:h + 1, cs]
            if own:
                s = jnp.where(causal, s, NEG)
                m_new = jnp.maximum(m_old, jnp.max(s, axis=0, keepdims=True))
                m_use = m_new
            else:
                on = sel_ref[h, pl.ds(j, 1), cs] > 0.0
                m_new = jnp.where(on, jnp.maximum(m_old, jnp.max(s, axis=0, keepdims=True)), m_old)
                m_use = jnp.where(on, m_new, -NEG)
            p = jnp.exp2(s - m_use).astype(BF16)
            pv = _dot(jnp.concatenate([v_ref[j, h * hd:(h + 1) * hd, :], ones], axis=0), p)
            acc_ref[h, :, cs] = jnp.exp2(m_old - m_new) * acc_ref[h, :, cs] + pv
            m_ref[h:h + 1, cs] = m_new
        return run

    def overlap(scores, finishes):
        for run in scores[:MOBA_LOOKAHEAD]:
            run()
        rest = scores[MOBA_LOOKAHEAD:]
        for run in finishes:
            run()
            if rest:
                rest.pop(0)()
        for run in rest:
            run()

    everyone = range(len(groups))
    overlap([score(0, 0, g) for g in everyone], [select(0), select(1)])

    def sweep(j0, n_blocks):
        for t in range(n_blocks):
            overlap([score((t + 1) % 2, j0 + t + 1, g) for g in everyone],
                    [finish(t % 2, j0 + t, g, False) for g in everyone])

    def double_trip(jj, carry):
        sweep(2 * halves * jj, 2 * halves)
        return carry

    def single_trip(jj, carry):
        sweep(2 * halves * (i // 2), halves)
        return carry

    lax.fori_loop(0, i // 2, double_trip, 0)
    lax.fori_loop(0, i % 2, single_trip, 0)

    causal = (lax.broadcasted_iota(jnp.int32, (blk, blk), 0)
              <= lax.broadcasted_iota(jnp.int32, (blk, blk), 1))
    base = i * halves
    for t in range(halves):
        later = [g for g in everyone if groups[g][1] > t]
        overlap([score((t + 1) % 2, base + t + 1, g) for g in later],
                [finish(t % 2, base + t, g, groups[g][1] == t) for g in everyone if groups[g][1] >= t])

    out_t = jnp.concatenate([acc_ref[h, :hd, :] * (1.0 / acc_ref[h, hd:hd + 1, :]) for h in range(2)], axis=0)
    o_ref[...] = out_t.T.astype(BF16)


def _moba(z, qt, vt, batch, seq, tq):
    blk = MOBA_BLOCK
    nb = seq // blk
    nq = seq // tq
    d = qt.shape[0]
    pairs = d // LANES
    return pl.pallas_call(
        functools.partial(_moba_kernel, nb=nb, blk=blk, tq=tq),
        grid=(batch, pairs, nq),
        in_specs=[
            pl.BlockSpec((LANES, tq), lambda b, p, i: (p, b * nq + i)),
            pl.BlockSpec((seq, LANES), lambda b, p, i: (b, p)),
            pl.BlockSpec((nb, LANES, blk), lambda b, p, i: (b, p, 0)),
        ],
        out_specs=pl.BlockSpec((tq, LANES), lambda b, p, i: (b * nq + i, p)),
        out_shape=jax.ShapeDtypeStruct((batch * seq, d), BF16),
        scratch_shapes=[
            pltpu.VMEM((nb, LANES), F32),
            pltpu.VMEM((2, nb, tq), F32),
            pltpu.VMEM((2, tq), F32),
            pltpu.VMEM((2, A_HEAD_DIM + SUM_ROWS, tq), F32),
            pltpu.VMEM((2, 2 * (tq // blk), blk, blk), F32),
        ],
        compiler_params=_params(("parallel", "parallel", "arbitrary")),
        name="moba",
    )(qt, z, vt)


def _hgrn_kernel(q_ref, f_ref, i_ref, og_ref, lbl_ref, ng_ref, o_ref, st_ref, *, lt, group):
    c_len, sub = HGRN_CHUNK, HGRN_SUB
    n_sub = c_len // sub
    dk = B_KEY_DIM
    heads = [slice(g * LANES, (g + 1) * LANES) for g in range(group)]

    @pl.when(pl.program_id(2) == 0)
    def _():
        st_ref[...] = jnp.zeros(st_ref.shape, F32)

    lg = lbl_ref[...]
    e = jnp.exp(lg - jnp.max(lg, axis=0, keepdims=True))
    lb = e[0:1, :] / jnp.sum(e, axis=0, keepdims=True)
    ng = ng_ref[...]

    tri = (lax.broadcasted_iota(jnp.int32, (c_len, c_len), 0)
           >= lax.broadcasted_iota(jnp.int32, (c_len, c_len), 1)).astype(BF16)
    width = group * dk
    half = sub // 2
    n_half = c_len // half
    rowi = lax.broadcasted_iota(jnp.int32, (n_half, half, width), 1)
    same_sub = (lax.broadcasted_iota(jnp.int32, (n_sub * half, n_sub * half), 0) // half
                == lax.broadcasted_iota(jnp.int32, (n_sub * half, n_sub * half), 1) // half)

    def step(c, states):
        rows = pl.ds(pl.multiple_of(c * c_len, c_len), c_len)
        qb = q_ref[rows, :].astype(F32)
        fb = f_ref[rows, :].astype(F32)
        v = i_ref[rows, :].astype(F32)
        og = og_ref[rows, :].astype(F32)

        q = qb * _sigmoid(qb) * (dk ** -0.5)
        fg = lb + (1.0 - lb) * _sigmoid(fb)
        logf = jnp.log2(fg)
        k = 1.0 - fg

        hi = logf.astype(BF16)
        r1 = logf - hi.astype(F32)
        mid = r1.astype(BF16)
        lo = (r1 - mid.astype(F32)).astype(BF16)
        b = _dot(tri, hi) + _dot(tri, mid) + _dot(tri, lo)

        qe = (q * jnp.exp2(b)).astype(BF16)
        o_inter = [_dot_nt(qe[:, hs], states[g].astype(BF16)) for g, hs in enumerate(heads)]

        v16 = v.astype(BF16)
        pair_a = []
        for s_i in range(1, n_sub):
            lo_r = s_i * sub
            ref_b = b[lo_r - 1:lo_r, :]
            qt = (q[lo_r:lo_r + sub, :] * jnp.exp2(b[lo_r:lo_r + sub, :] - ref_b)).astype(BF16)
            kt = (k[:lo_r, :] * jnp.exp2(ref_b - b[:lo_r, :])).astype(BF16)
            pair_a.append([_dot_nt(qt[:, hs], kt[:, hs]).astype(BF16) for hs in heads])
        pair_o = [[_dot(pair_a[s_i - 1][g], v16[:s_i * sub, hs]) for s_i in range(1, n_sub)]
                  for g, hs in enumerate(heads)]

        b3 = b.reshape(n_sub, sub, width)
        ref_h = b3[:, half - 1:half, :]
        q_h = q.reshape(n_sub, sub, width)[:, half:, :] * jnp.exp2(b3[:, half:, :] - ref_h)
        k_h = k.reshape(n_sub, sub, width)[:, :half, :] * jnp.exp2(ref_h - b3[:, :half, :])
        q_h = q_h.reshape(n_sub * half, width).astype(BF16)
        k_h = k_h.reshape(n_sub * half, width).astype(BF16)
        v_h = v.reshape(n_sub, sub, width)[:, :half, :].reshape(n_sub * half, width).astype(BF16)
        half_a = [jnp.where(same_sub, _dot_nt(q_h[:, hs], k_h[:, hs]), 0.0).astype(BF16) for hs in heads]
        half_o = [_dot(half_a[g], v_h[:, hs]) for g, hs in enumerate(heads)]

        b_last = b[c_len - 1:c_len, :]
        kl = (k * jnp.exp2(b_last - b)).astype(BF16)
        decay = jnp.exp2(b_last)
        new_states = [states[g] * decay[:, hs] + _dot(v[:, hs].T.astype(BF16), kl[:, hs])
                      for g, hs in enumerate(heads)]

        b8 = b.reshape(n_half, half, width)
        q8 = q.reshape(n_half, half, width)
        k8 = k.reshape(n_half, half, width)
        v8 = v.reshape(n_half, half, width)
        od = [jnp.zeros((n_half, half, dk), F32) for _ in heads]
        for s_o in range(half):
            dec = jnp.exp2(jnp.where(rowi >= s_o, b8 - b8[:, s_o:s_o + 1, :], NEG))
            gq = q8 * dec * k8[:, s_o:s_o + 1, :]
            for g, hs in enumerate(heads):
                od[g] = od[g] + jnp.sum(gq[:, :, hs], axis=-1, keepdims=True) * v8[:, s_o:s_o + 1, hs]

        gate = og * _sigmoid(og)
        for g, hs in enumerate(heads):
            o = o_inter[g] + jnp.concatenate([jnp.zeros((sub, dk), F32)] + pair_o[g], axis=0)
            second = jnp.concatenate([jnp.zeros((n_sub, half, dk), F32), half_o[g].reshape(n_sub, half, dk)], axis=1)
            o = o + second.reshape(c_len, dk) + od[g].reshape(c_len, dk)
            o = o * lax.rsqrt(jnp.mean(o * o, axis=-1, keepdims=True) + EPS) * ng[:, hs]
            o_ref[rows, hs] = (o * gate[:, hs]).astype(BF16)
        return new_states

    def trip(n, carry):
        states = [st_ref[g] for g in range(group)]
        for u in range(HGRN_UNROLL):
            states = step(n * HGRN_UNROLL + u, states)
        for g in range(group):
            st_ref[g] = states[g]
        return carry

    lax.fori_loop(0, lt // (c_len * HGRN_UNROLL), trip, 0)


def _hgrn(z, lb_logits, norm_g, batch, seq, lt, group):
    nt = seq // lt
    width = group * LANES
    d = B_HEADS * LANES
    seg = lambda off: pl.BlockSpec((lt, width), lambda b, hh, t: (b * nt + t, off * (d // width) + hh))
    return pl.pallas_call(
        functools.partial(_hgrn_kernel, lt=lt, group=group),
        grid=(batch, B_HEADS // group, nt),
        in_specs=[
            seg(1), seg(2), seg(3), seg(4),
            pl.BlockSpec((lb_logits.shape[0], width), lambda b, hh, t: (0, hh)),
            pl.BlockSpec((1, width), lambda b, hh, t: (0, hh)),
        ],
        out_specs=pl.BlockSpec((lt, width), lambda b, hh, t: (b * nt + t, hh)),
        out_shape=jax.ShapeDtypeStruct((batch * seq, d), BF16),
        scratch_shapes=[pltpu.VMEM((group, LANES, B_KEY_DIM), F32)],
        compiler_params=_params(("parallel", "parallel", "arbitrary")),
        name="hgrn",
    )(z, z, z, z, lb_logits, norm_g)


def _merge_kernel(x_ref, a_ref, hg_ref, ga_ref, gb_ref, wa_ref, wb_ref, wo_ref, gx_ref, wq_ref,
                  km_ref, vm_ref, wox_ref, o_ref):
    ya = _dot(a_ref[...], wa_ref[...])
    yb = _dot(hg_ref[...], wb_ref[...])
    mg = _sigmoid(ga_ref[...].astype(F32)) * ya + _sigmoid(gb_ref[...].astype(F32)) * yb
    h = x_ref[...] + _dot(mg.astype(BF16), wo_ref[...])

    hn = _rms(h, gx_ref[...]).astype(BF16)
    q = _dot(hn, wq_ref[...])
    d = q.shape[1]
    xd = d // X_HEADS
    outs = []
    for hh in range(X_HEADS):
        sl = slice(hh * xd, (hh + 1) * xd)
        s = _dot_nt(q[:, sl].astype(BF16), km_ref[:, sl]) * (xd ** -0.5)
        p = jnp.exp(s - jnp.max(s, axis=-1, keepdims=True))
        oh = _dot(p.astype(BF16), vm_ref[:, sl])
        outs.append(oh * (1.0 / jnp.sum(p, axis=-1, keepdims=True)))
    o = jnp.concatenate(outs, axis=-1)
    o_ref[...] = h + _dot(o.astype(BF16), wox_ref[...])


def _merge(x2d, attn, hg, z, wa, wb, wo, gx, wq, kvm, wox, batch, seq, n_mem, tm):
    d = x2d.shape[1]
    nt = seq // tm
    gcol = (z.shape[1] - 2 * d) // d
    row = lambda b, t: (b * nt + t, 0)
    full = lambda b, t: (0, 0)
    wspec = pl.BlockSpec((d, d), full)
    return pl.pallas_call(
        _merge_kernel,
        grid=(batch, nt),
        in_specs=[
            pl.BlockSpec((tm, d), row),
            pl.BlockSpec((tm, d), row),
            pl.BlockSpec((tm, d), row),
            pl.BlockSpec((tm, d), lambda b, t: (b * nt + t, gcol)),
            pl.BlockSpec((tm, d), lambda b, t: (b * nt + t, gcol + 1)),
            wspec, wspec, wspec,
            pl.BlockSpec((1, d), full),
            wspec,
            pl.BlockSpec((n_mem, d), lambda b, t: (b, 0)),
            pl.BlockSpec((n_mem, d), lambda b, t: (b, 1)),
            wspec,
        ],
        out_specs=pl.BlockSpec((tm, d), row),
        out_shape=jax.ShapeDtypeStruct(x2d.shape, F32),
        compiler_params=_params(("parallel", "parallel")),
        name="merge",
    )(x2d, attn, hg, z, z, wa, wb, wo, gx, wq, kvm, kvm, wox)


def _ffn_kernel(h_ref, g_ref, wi_ref, wo_ref, gf_ref, o_ref, *, hidden, bounds):
    h = h_ref[...]
    hn = _rms(h, g_ref[...]).astype(BF16)
    acc = h
    for lo, hi in bounds:
        gate = _dot(hn, wi_ref[:, lo:hi])
        up = _dot(hn, wi_ref[:, hidden + lo:hidden + hi])
        act = (gate * _sigmoid(gate) * up).astype(BF16)
        acc = acc + _dot(act, wo_ref[lo:hi, :])
    o_ref[...] = _rms(acc, gf_ref[...])


def _ffn(h2d, g, wi, wo, gf, tm):
    t, d = h2d.shape
    hidden = wo.shape[0]
    step = 1024
    bounds = tuple((lo, min(lo + step, hidden)) for lo in range(0, hidden, step))
    full = lambda i: (0, 0)
    return pl.pallas_call(
        functools.partial(_ffn_kernel, hidden=hidden, bounds=bounds),
        grid=(t // tm,),
        in_specs=[
            pl.BlockSpec((tm, d), lambda i: (i, 0)),
            pl.BlockSpec((1, d), full),
            pl.BlockSpec(wi.shape, full, pipeline_mode=pl.Buffered(1)),
            pl.BlockSpec(wo.shape, full, pipeline_mode=pl.Buffered(1)),
            pl.BlockSpec((1, d), full),
        ],
        out_specs=pl.BlockSpec((tm, d), lambda i: (i, 0)),
        out_shape=jax.ShapeDtypeStruct(h2d.shape, F32),
        compiler_params=_params(("parallel",)),
        name="ffn",
    )(h2d, g, wi, wo, gf)


def _rope_tables(positions):
    half = A_HEAD_DIM // 2
    inv_freq = jnp.power(ROPE_THETA, -jnp.arange(0, A_HEAD_DIM, 2, dtype=F32) / A_HEAD_DIM)
    pos = positions.astype(F32).reshape(-1)
    lane = jnp.arange(LANES)
    ang = pos[:, None] * inv_freq[lane % half][None, :]
    sign = jnp.where((lane % A_HEAD_DIM) < half, -1.0, 1.0).astype(F32)
    ang_t = inv_freq[:, None] * pos[None, :]
    return jnp.cos(ang), jnp.sin(ang) * sign, jnp.cos(ang_t), jnp.sin(ang_t)


def kernel(x, mem, positions, norm_mix_g, w_in, hgrn_lb_logits, hgrn_norm_g, w_br_attn, w_br_hgrn, w_out,
           norm_x_g, norm_mem_g, wq_x, wkv_x, wo_x, norm_ffn_g, w_ffn_in, w_ffn_out, final_norm_g):
    batch, seq, d = x.shape
    n_mem = mem.shape[1]
    depth = w_in.shape[0]
    assert depth == 1 and d == A_HEADS * A_HEAD_DIM == B_HEADS * LANES and B_KEY_DIM == LANES
    tm = min(512, seq)
    tq = MOBA_QUERY_BLOCKS * MOBA_BLOCK
    assert seq % tq == 0 and seq % tm == 0 and w_in.shape[2] == 9 * d
    lt = min(1024, seq)
    x2d = x.reshape(batch * seq, d)
    cos_t, sin_t, cos_tt, sin_tt = _rope_tables(positions)
    row = lambda g: g.reshape(1, -1).astype(F32)
    bf = lambda w: w.astype(BF16)

    w_in16 = bf(w_in[0])
    wt_qv = jnp.stack([w_in16[:, :d].T, w_in16[:, 2 * d:3 * d].T])

    kvm = _memkv(mem.reshape(batch * n_mem, d), row(norm_mem_g[0]), bf(wkv_x[0]), n_mem)
    z, qt, vt = _inproj(x2d, row(norm_mix_g[0]), w_in16, wt_qv, cos_t, sin_t, cos_tt, sin_tt, tm)
    attn = _moba(z, qt, vt, batch, seq, tq)
    hg = _hgrn(z, hgrn_lb_logits.astype(F32), row(hgrn_norm_g[0]), batch, seq, lt, HGRN_GROUP)
    h = _merge(x2d, attn, hg, z, bf(w_br_attn[0]), bf(w_br_hgrn[0]), bf(w_out[0]), row(norm_x_g[0]),
               bf(wq_x[0]), kvm, bf(wo_x[0]), batch, seq, n_mem, tm)
    out = _ffn(h, row(norm_ffn_g[0]), bf(w_ffn_in[0]), bf(w_ffn_out[0]), row(final_norm_g), tm)
    return out.reshape(batch, seq, d)
```

```python
import functools

import jax
import jax.numpy as jnp
from jax import lax
from jax.experimental import pallas as pl
from jax.experimental.pallas import tpu as pltpu

F32 = jnp.float32
BF16 = jnp.bfloat16

EPS = 1e-6
NEG = -1e30
ROPE_THETA = 10000.0

A_HEADS = 16
A_HEAD_DIM = 64
MOBA_BLOCK = 256
MOBA_TOPK = 3
MOBA_QUERY_BLOCKS = 4
MOBA_LOOKAHEAD = 3
B_HEADS = 8
B_KEY_DIM = 128
X_HEADS = 4
HGRN_CHUNK = 64
HGRN_SUB = 16
HGRN_GROUP = 8
HGRN_UNROLL = 4

LANES = 128
SUM_ROWS = 16
LOG2_E = 1.4426950408889634
VMEM_LIMIT = 56 * 1024 * 1024


def _params(sem):
    return pltpu.CompilerParams(dimension_semantics=sem, vmem_limit_bytes=VMEM_LIMIT)


def _rms(x, g):
    return x * lax.rsqrt(jnp.mean(x * x, axis=-1, keepdims=True) + EPS) * g


def _dot(a, b):
    return jnp.dot(a, b, preferred_element_type=F32)


def _dot_nt(a, b):
    return lax.dot_general(a, b, (((1,), (1,)), ((), ())), preferred_element_type=F32)


def _sigmoid(x):
    return 1.0 / (1.0 + jnp.exp(-x))


def _memkv_kernel(mem_ref, g_ref, w_ref, o_ref):
    y = _rms(mem_ref[...], g_ref[...])
    o_ref[...] = _dot(y.astype(BF16), w_ref[...]).astype(BF16)


def _memkv(mem2d, g, w, n_mem):
    rows, d = mem2d.shape
    return pl.pallas_call(
        _memkv_kernel,
        grid=(rows // n_mem,),
        in_specs=[
            pl.BlockSpec((n_mem, d), lambda b: (b, 0)),
            pl.BlockSpec((1, d), lambda b: (0, 0)),
            pl.BlockSpec(w.shape, lambda b: (0, 0)),
        ],
        out_specs=pl.BlockSpec((n_mem, w.shape[1]), lambda b: (b, 0)),
        out_shape=jax.ShapeDtypeStruct((rows, w.shape[1]), BF16),
        compiler_params=_params(("parallel",)),
        name="memkv",
    )(mem2d, g, w)


def _inproj_kernel(x_ref, g_ref, w_ref, wt_ref, cos_ref, sin_ref, cost_ref, sint_ref, z_ref, qt_ref, vt_ref, *, tm):
    hd, half = A_HEAD_DIM, A_HEAD_DIM // 2
    d = x_ref.shape[1]
    xn = _rms(x_ref[...], g_ref[...]).astype(BF16)

    acc = _dot_nt(wt_ref[0], xn)
    cos = cost_ref[...]
    sin = sint_ref[...]
    scale = hd ** -0.5 * LOG2_E
    for h in range(d // hd):
        t1 = acc[h * hd:h * hd + half, :]
        t2 = acc[h * hd + half:(h + 1) * hd, :]
        qt_ref[h * hd:h * hd + half, :] = ((t1 * cos - t2 * sin) * scale).astype(BF16)
        qt_ref[h * hd + half:(h + 1) * hd, :] = ((t2 * cos + t1 * sin) * scale).astype(BF16)

    acc = _dot(xn, w_ref[:, d:2 * d])
    cos = cos_ref[...]
    sin = sin_ref[...]
    lane = lax.broadcasted_iota(jnp.int32, cos.shape, 1)
    low = (lane % hd) < half
    for c in range(d // LANES):
        t = acc[:, c * LANES:(c + 1) * LANES]
        rot = jnp.where(low, pltpu.roll(t, LANES - half, 1), pltpu.roll(t, half, 1))
        z_ref[:, c * LANES:(c + 1) * LANES] = (t * cos + rot * sin).astype(BF16)

    acc = _dot_nt(wt_ref[1], xn).astype(BF16)
    for n in range(tm // MOBA_BLOCK):
        vt_ref[n] = acc[:, n * MOBA_BLOCK:(n + 1) * MOBA_BLOCK]

    for s in range(3, w_ref.shape[1] // d):
        z_ref[:, (s - 2) * d:(s - 1) * d] = _dot(xn, w_ref[:, s * d:(s + 1) * d]).astype(BF16)


def _inproj(x2d, g, w, wt, cos_t, sin_t, cos_tt, sin_tt, tm):
    t, d = x2d.shape
    blk = MOBA_BLOCK
    full = lambda i: (0, 0)
    return pl.pallas_call(
        functools.partial(_inproj_kernel, tm=tm),
        grid=(t // tm,),
        in_specs=[
            pl.BlockSpec((tm, d), lambda i: (i, 0)),
            pl.BlockSpec((1, d), full),
            pl.BlockSpec(w.shape, full, pipeline_mode=pl.Buffered(1)),
            pl.BlockSpec(wt.shape, lambda i: (0, 0, 0), pipeline_mode=pl.Buffered(1)),
            pl.BlockSpec((tm, LANES), lambda i: (i, 0)),
            pl.BlockSpec((tm, LANES), lambda i: (i, 0)),
            pl.BlockSpec((A_HEAD_DIM // 2, tm), lambda i: (0, i)),
            pl.BlockSpec((A_HEAD_DIM // 2, tm), lambda i: (0, i)),
        ],
        out_specs=[
            pl.BlockSpec((tm, w.shape[1] - 2 * d), lambda i: (i, 0)),
            pl.BlockSpec((d, tm), lambda i: (0, i)),
            pl.BlockSpec((tm // blk, d, blk), lambda i: (i, 0, 0)),
        ],
        out_shape=[
            jax.ShapeDtypeStruct((t, w.shape[1] - 2 * d), BF16),
            jax.ShapeDtypeStruct((d, t), BF16),
            jax.ShapeDtypeStruct((t // blk, d, blk), BF16),
        ],
        compiler_params=_params(("parallel",)),
        name="inproj",
    )(x2d, g, w, wt, cos_t, sin_t, cos_tt, sin_tt)


def _moba_kernel(q_ref, k_ref, v_ref, o_ref, km_ref, sel_ref, m_ref, acc_ref, s_ref, *, nb, blk, tq):
    i = pl.program_id(2)
    hd = A_HEAD_DIM
    halves = tq // blk

    @pl.when(i == 0)
    def _prep():
        for n in range(nb):
            km_ref[n:n + 1, :] = jnp.mean(k_ref[n * blk:(n + 1) * blk, :].astype(F32), axis=0, keepdims=True)

    q_t = q_ref[...]
    row = lax.broadcasted_iota(jnp.int32, q_t.shape, 0)
    zero = jnp.zeros_like(q_t)
    qz = [jnp.where(row < hd, q_t, zero), jnp.where(row >= hd, q_t, zero)]

    km = km_ref[...]
    km_hi = km.astype(BF16)
    km_lo = (km - km_hi.astype(F32)).astype(BF16)
    blk_id = lax.broadcasted_iota(jnp.int32, (nb, tq), 0)
    q_blk = i * halves + lax.broadcasted_iota(jnp.int32, (nb, tq), 1) // blk
    gates = [_dot(km_hi, qz[h]) + _dot(km_lo, qz[h]) for h in range(2)]

    def select(h):
        def run():
            g = jnp.where(blk_id < q_blk, gates[h], -jnp.inf)
            sel = jnp.zeros((nb, tq), F32)
            for _ in range(MOBA_TOPK):
                mx = jnp.max(g, axis=0, keepdims=True)
                first = jnp.min(jnp.where(g == mx, blk_id, nb), axis=0, keepdims=True)
                first = jnp.where(mx > -jnp.inf, first, -1)
                pick = blk_id == first
                sel = jnp.where(pick, 1.0, sel)
                g = jnp.where(pick, -jnp.inf, g)
            sel_ref[h] = sel
        return run

    m_ref[...] = jnp.full(m_ref.shape, NEG, F32)
    acc_ref[...] = jnp.zeros(acc_ref.shape, F32)
    ones = jnp.ones((SUM_ROWS, blk), BF16)

    groups = [(h, c) for h in range(2) for c in range(halves)]

    def score(slot, j, g):
        def run():
            h, c = groups[g]
            kj = k_ref[pl.ds(pl.multiple_of(j * blk, blk), blk), :]
            s_ref[slot, g] = _dot(kj, qz[h][:, c * blk:(c + 1) * blk])
        return run

    def finish(slot, j, g, own):
        def run():
            h, c = groups[g]
            cs = slice(c * blk, (c + 1) * blk)
            s = s_ref[slot, g]
            m_old = m_ref[h:h + 1, cs]
            if own:
                s = jnp.where(causal, s, NEG)
                m_new = jnp.maximum(m_old, jnp.max(s, axis=0, keepdims=True))
                m_use = m_new
            else:
                on = sel_ref[h, pl.ds(j, 1), cs] > 0.0
                m_new = jnp.where(on, jnp.maximum(m_old, jnp.max(s, axis=0, keepdims=True)), m_old)
                m_use = jnp.where(on, m_new, -NEG)
            p = jnp.exp2(s - m_use).astype(BF16)
            pv = _dot(jnp.concatenate([v_ref[j, h * hd:(h + 1) * hd, :], ones], axis=0), p)
            acc_ref[h, :, cs] = jnp.exp2(m_old - m_new) * acc_ref[h, :, cs] + pv
            m_ref[h:h + 1, cs] = m_new
        return run

    def overlap(scores, finishes):
        for run in scores[:MOBA_LOOKAHEAD]:
            run()
        rest = scores[MOBA_LOOKAHEAD:]
        for run in finishes:
            run()
            if rest:
                rest.pop(0)()
        for run in rest:
            run()

    everyone = range(len(groups))
    overlap([score(0, 0, g) for g in everyone], [select(0), select(1)])

    def sweep(j0, n_blocks):
        for t in range(n_blocks):
            overlap([score((t + 1) % 2, j0 + t + 1, g) for g in everyone],
                    [finish(t % 2, j0 + t, g, False) for g in everyone])

    def quad_trip(jj, carry):
        sweep(4 * halves * jj, 4 * halves)
        return carry

    def double_trip(jj, carry):
        sweep(4 * halves * (i // 4), 2 * halves)
        return carry

    def single_trip(jj, carry):
        sweep(2 * halves * (i // 2), halves)
        return carry

    lax.fori_loop(0, i // 4, quad_trip, 0)
    lax.fori_loop(0, (i % 4) // 2, double_trip, 0)
    lax.fori_loop(0, i % 2, single_trip, 0)

    causal = (lax.broadcasted_iota(jnp.int32, (blk, blk), 0)
              <= lax.broadcasted_iota(jnp.int32, (blk, blk), 1))
    base = i * halves
    for t in range(halves):
        later = [g for g in everyone if groups[g][1] > t]
        overlap([score((t + 1) % 2, base + t + 1, g) for g in later],
                [finish(t % 2, base + t, g, groups[g][1] == t) for g in everyone if groups[g][1] >= t])

    out_t = jnp.concatenate([acc_ref[h, :hd, :] * (1.0 / acc_ref[h, hd:hd + 1, :]) for h in range(2)], axis=0)
    o_ref[...] = out_t.T.astype(BF16)


def _moba(z, qt, vt, batch, seq, tq):
    blk = MOBA_BLOCK
    nb = seq // blk
    nq = seq // tq
    d = qt.shape[0]
    pairs = d // LANES
    return pl.pallas_call(
        functools.partial(_moba_kernel, nb=nb, blk=blk, tq=tq),
        grid=(batch, pairs, nq),
        in_specs=[
            pl.BlockSpec((LANES, tq), lambda b, p, i: (p, b * nq + i)),
            pl.BlockSpec((seq, LANES), lambda b, p, i: (b, p)),
            pl.BlockSpec((nb, LANES, blk), lambda b, p, i: (b, p, 0)),
        ],
        out_specs=pl.BlockSpec((tq, LANES), lambda b, p, i: (b * nq + i, p)),
        out_shape=jax.ShapeDtypeStruct((batch * seq, d), BF16),
        scratch_shapes=[
            pltpu.VMEM((nb, LANES), F32),
            pltpu.VMEM((2, nb, tq), F32),
            pltpu.VMEM((2, tq), F32),
            pltpu.VMEM((2, A_HEAD_DIM + SUM_ROWS, tq), F32),
            pltpu.VMEM((2, 2 * (tq // blk), blk, blk), F32),
        ],
        compiler_params=_params(("parallel", "parallel", "arbitrary")),
        name="moba",
    )(qt, z, vt)


def _hgrn_kernel(q_ref, f_ref, i_ref, og_ref, lbl_ref, ng_ref, o_ref, st_ref, *, lt, group):
    c_len, sub = HGRN_CHUNK, HGRN_SUB
    n_sub = c_len // sub
    dk = B_KEY_DIM
    heads = [slice(g * LANES, (g + 1) * LANES) for g in range(group)]

    @pl.when(pl.program_id(2) == 0)
    def _():
        st_ref[...] = jnp.zeros(st_ref.shape, F32)

    lg = lbl_ref[...]
    e = jnp.exp(lg - jnp.max(lg, axis=0, keepdims=True))
    lb = e[0:1, :] / jnp.sum(e, axis=0, keepdims=True)
    ng = ng_ref[...]

    tri = (lax.broadcasted_iota(jnp.int32, (c_len, c_len), 0)
           >= lax.broadcasted_iota(jnp.int32, (c_len, c_len), 1)).astype(BF16)
    width = group * dk
    half = sub // 2
    n_half = c_len // half
    rowi = lax.broadcasted_iota(jnp.int32, (n_half, half, width), 1)
    same_sub = (lax.broadcasted_iota(jnp.int32, (n_sub * half, n_sub * half), 0) // half
                == lax.broadcasted_iota(jnp.int32, (n_sub * half, n_sub * half), 1) // half)

    def step(c, states):
        rows = pl.ds(pl.multiple_of(c * c_len, c_len), c_len)
        qb = q_ref[rows, :].astype(F32)
        fb = f_ref[rows, :].astype(F32)
        v = i_ref[rows, :].astype(F32)
        og = og_ref[rows, :].astype(F32)

        q = qb * _sigmoid(qb) * (dk ** -0.5)
        fg = lb + (1.0 - lb) * _sigmoid(fb)
        logf = jnp.log2(fg)
        k = 1.0 - fg

        hi = logf.astype(BF16)
        r1 = logf - hi.astype(F32)
        mid = r1.astype(BF16)
        lo = (r1 - mid.astype(F32)).astype(BF16)
        b = _dot(tri, hi) + _dot(tri, mid) + _dot(tri, lo)

        qe = (q * jnp.exp2(b)).astype(BF16)
        o_inter = [_dot_nt(qe[:, hs], states[g].astype(BF16)) for g, hs in enumerate(heads)]

        v16 = v.astype(BF16)
        pair_a = []
        for s_i in range(1, n_sub):
            lo_r = s_i * sub
            ref_b = b[lo_r - 1:lo_r, :]
            qt = (q[lo_r:lo_r + sub, :] * jnp.exp2(b[lo_r:lo_r + sub, :] - ref_b)).astype(BF16)
            kt = (k[:lo_r, :] * jnp.exp2(ref_b - b[:lo_r, :])).astype(BF16)
            pair_a.append([_dot_nt(qt[:, hs], kt[:, hs]).astype(BF16) for hs in heads])
        pair_o = [[_dot(pair_a[s_i - 1][g], v16[:s_i * sub, hs]) for s_i in range(1, n_sub)]
                  for g, hs in enumerate(heads)]

        b3 = b.reshape(n_sub, sub, width)
        ref_h = b3[:, half - 1:half, :]
        q_h = q.reshape(n_sub, sub, width)[:, half:, :] * jnp.exp2(b3[:, half:, :] - ref_h)
        k_h = k.reshape(n_sub, sub, width)[:, :half, :] * jnp.exp2(ref_h - b3[:, :half, :])
        q_h = q_h.reshape(n_sub * half, width).astype(BF16)
        k_h = k_h.reshape(n_sub * half, width).astype(BF16)
        v_h = v.reshape(n_sub, sub, width)[:, :half, :].reshape(n_sub * half, width).astype(BF16)
        half_a = [jnp.where(same_sub, _dot_nt(q_h[:, hs], k_h[:, hs]), 0.0).astype(BF16) for hs in heads]
        half_o = [_dot(half_a[g], v_h[:, hs]) for g, hs in enumerate(heads)]

        b_last = b[c_len - 1:c_len, :]
        kl = (k * jnp.exp2(b_last - b)).astype(BF16)
        decay = jnp.exp2(b_last)
        new_states = [states[g] * decay[:, hs] + _dot(v[:, hs].T.astype(BF16), kl[:, hs])
                      for g, hs in enumerate(heads)]

        b8 = b.reshape(n_half, half, width)
        q8 = q.reshape(n_half, half, width)
        k8 = k.reshape(n_half, half, width)
        v8 = v.reshape(n_half, half, width)
        od = [jnp.zeros((n_half, half, dk), F32) for _ in heads]
        for s_o in range(half):
            dec = jnp.exp2(jnp.where(rowi >= s_o, b8 - b8[:, s_o:s_o + 1, :], NEG))
            gq = q8 * dec * k8[:, s_o:s_o + 1, :]
            for g, hs in enumerate(heads):
                od[g] = od[g] + jnp.sum(gq[:, :, hs], axis=-1, keepdims=True) * v8[:, s_o:s_o + 1, hs]

        gate = og * _sigmoid(og)
        for g, hs in enumerate(heads):
            o = o_inter[g] + jnp.concatenate([jnp.zeros((sub, dk), F32)] + pair_o[g], axis=0)
            second = jnp.concatenate([jnp.zeros((n_sub, half, dk), F32), half_o[g].reshape(n_sub, half, dk)], axis=1)
            o = o + second.reshape(c_len, dk) + od[g].reshape(c_len, dk)
            o = o * lax.rsqrt(jnp.mean(o * o, axis=-1, keepdims=True) + EPS) * ng[:, hs]
            o_ref[rows, hs] = (o * gate[:, hs]).astype(BF16)
        return new_states

    def trip(n, carry):
        states = [st_ref[g] for g in range(group)]
        for u in range(HGRN_UNROLL):
            states = step(n * HGRN_UNROLL + u, states)
        for g in range(group):
            st_ref[g] = states[g]
        return carry

    lax.fori_loop(0, lt // (c_len * HGRN_UNROLL), trip, 0)


def _hgrn(z, lb_logits, norm_g, batch, seq, lt, group):
    nt = seq // lt
    width = group * LANES
    d = B_HEADS * LANES
    seg = lambda off: pl.BlockSpec((lt, width), lambda b, hh, t: (b * nt + t, off * (d // width) + hh))
    return pl.pallas_call(
        functools.partial(_hgrn_kernel, lt=lt, group=group),
        grid=(batch, B_HEADS // group, nt),
        in_specs=[
            seg(1), seg(2), seg(3), seg(4),
            pl.BlockSpec((lb_logits.shape[0], width), lambda b, hh, t: (0, hh)),
            pl.BlockSpec((1, width), lambda b, hh, t: (0, hh)),
        ],
        out_specs=pl.BlockSpec((lt, width), lambda b, hh, t: (b * nt + t, hh)),
        out_shape=jax.ShapeDtypeStruct((batch * seq, d), BF16),
        scratch_shapes=[pltpu.VMEM((group, LANES, B_KEY_DIM), F32)],
        compiler_params=_params(("parallel", "parallel", "arbitrary")),
        name="hgrn",
    )(z, z, z, z, lb_logits, norm_g)


def _merge_kernel(x_ref, a_ref, hg_ref, ga_ref, gb_ref, wa_ref, wb_ref, wo_ref, gx_ref, wq_ref,
                  km_ref, vm_ref, wox_ref, o_ref):
    ya = _dot(a_ref[...], wa_ref[...])
    yb = _dot(hg_ref[...], wb_ref[...])
    mg = _sigmoid(ga_ref[...].astype(F32)) * ya + _sigmoid(gb_ref[...].astype(F32)) * yb
    h = x_ref[...] + _dot(mg.astype(BF16), wo_ref[...])

    hn = _rms(h, gx_ref[...]).astype(BF16)
    q = _dot(hn, wq_ref[...])
    d = q.shape[1]
    xd = d // X_HEADS
    outs = []
    for hh in range(X_HEADS):
        sl = slice(hh * xd, (hh + 1) * xd)
        s = _dot_nt(q[:, sl].astype(BF16), km_ref[:, sl]) * (xd ** -0.5)
        p = jnp.exp(s - jnp.max(s, axis=-1, keepdims=True))
        oh = _dot(p.astype(BF16), vm_ref[:, sl])
        outs.append(oh * (1.0 / jnp.sum(p, axis=-1, keepdims=True)))
    o = jnp.concatenate(outs, axis=-1)
    o_ref[...] = h + _dot(o.astype(BF16), wox_ref[...])


def _merge(x2d, attn, hg, z, wa, wb, wo, gx, wq, kvm, wox, batch, seq, n_mem, tm):
    d = x2d.shape[1]
    nt = seq // tm
    gcol = (z.shape[1] - 2 * d) // d
    row = lambda b, t: (b * nt + t, 0)
    full = lambda b, t: (0, 0)
    wspec = pl.BlockSpec((d, d), full)
    return pl.pallas_call(
        _merge_kernel,
        grid=(batch, nt),
        in_specs=[
            pl.BlockSpec((tm, d), row),
            pl.BlockSpec((tm, d), row),
            pl.BlockSpec((tm, d), row),
            pl.BlockSpec((tm, d), lambda b, t: (b * nt + t, gcol)),
            pl.BlockSpec((tm, d), lambda b, t: (b * nt + t, gcol + 1)),
            wspec, wspec, wspec,
            pl.BlockSpec((1, d), full),
            wspec,
            pl.BlockSpec((n_mem, d), lambda b, t: (b, 0)),
            pl.BlockSpec((n_mem, d), lambda b, t: (b, 1)),
            wspec,
        ],
        out_specs=pl.BlockSpec((tm, d), row),
        out_shape=jax.ShapeDtypeStruct(x2d.shape, F32),
        compiler_params=_params(("parallel", "parallel")),
        name="merge",
    )(x2d, attn, hg, z, z, wa, wb, wo, gx, wq, kvm, kvm, wox)


def _ffn_kernel(h_ref, g_ref, wi_ref, wo_ref, gf_ref, o_ref, *, hidden, bounds):
    h = h_ref[...]
    hn = _rms(h, g_ref[...]).astype(BF16)
    acc = h
    for lo, hi in bounds:
        gate = _dot(hn, wi_ref[:, lo:hi])
        up = _dot(hn, wi_ref[:, hidden + lo:hidden + hi])
        act = (gate * _sigmoid(gate) * up).astype(BF16)
        acc = acc + _dot(act, wo_ref[lo:hi, :])
    o_ref[...] = _rms(acc, gf_ref[...])


def _ffn(h2d, g, wi, wo, gf, tm):
    t, d = h2d.shape
    hidden = wo.shape[0]
    step = 1024
    bounds = tuple((lo, min(lo + step, hidden)) for lo in range(0, hidden, step))
    full = lambda i: (0, 0)
    return pl.pallas_call(
        functools.partial(_ffn_kernel, hidden=hidden, bounds=bounds),
        grid=(t // tm,),
        in_specs=[
            pl.BlockSpec((tm, d), lambda i: (i, 0)),
            pl.BlockSpec((1, d), full),
            pl.BlockSpec(wi.shape, full, pipeline_mode=pl.Buffered(1)),
            pl.BlockSpec(wo.shape, full, pipeline_mode=pl.Buffered(1)),
            pl.BlockSpec((1, d), full),
        ],
        out_specs=pl.BlockSpec((tm, d), lambda i: (i, 0)),
        out_shape=jax.ShapeDtypeStruct(h2d.shape, F32),
        compiler_params=_params(("parallel",)),
        name="ffn",
    )(h2d, g, wi, wo, gf)


def _rope_tables(positions):
    half = A_HEAD_DIM // 2
    inv_freq = jnp.power(ROPE_THETA, -jnp.arange(0, A_HEAD_DIM, 2, dtype=F32) / A_HEAD_DIM)
    pos = positions.astype(F32).reshape(-1)
    lane = jnp.arange(LANES)
    ang = pos[:, None] * inv_freq[lane % half][None, :]
    sign = jnp.where((lane % A_HEAD_DIM) < half, -1.0, 1.0).astype(F32)
    ang_t = inv_freq[:, None] * pos[None, :]
    return jnp.cos(ang), jnp.sin(ang) * sign, jnp.cos(ang_t), jnp.sin(ang_t)


def kernel(x, mem, positions, norm_mix_g, w_in, hgrn_lb_logits, hgrn_norm_g, w_br_attn, w_br_hgrn, w_out,
           norm_x_g, norm_mem_g, wq_x, wkv_x, wo_x, norm_ffn_g, w_ffn_in, w_ffn_out, final_norm_g):
    batch, seq, d = x.shape
    n_mem = mem.shape[1]
    depth = w_in.shape[0]
    assert depth == 1 and d == A_HEADS * A_HEAD_DIM == B_HEADS * LANES and B_KEY_DIM == LANES
    tm = min(512, seq)
    tq = MOBA_QUERY_BLOCKS * MOBA_BLOCK
    assert seq % tq == 0 and seq % tm == 0 and w_in.shape[2] == 9 * d
    lt = min(1024, seq)
    x2d = x.reshape(batch * seq, d)
    cos_t, sin_t, cos_tt, sin_tt = _rope_tables(positions)
    row = lambda g: g.reshape(1, -1).astype(F32)
    bf = lambda w: w.astype(BF16)

    w_in16 = bf(w_in[0])
    wt_qv = jnp.stack([w_in16[:, :d].T, w_in16[:, 2 * d:3 * d].T])

    kvm = _memkv(mem.reshape(batch * n_mem, d), row(norm_mem_g[0]), bf(wkv_x[0]), n_mem)
    z, qt, vt = _inproj(x2d, row(norm_mix_g[0]), w_in16, wt_qv, cos_t, sin_t, cos_tt, sin_tt, tm)
    attn = _moba(z, qt, vt, batch, seq, tq)
    hg = _hgrn(z, hgrn_lb_logits.astype(F32), row(hgrn_norm_g[0]), batch, seq, lt, HGRN_GROUP)
    h = _merge(x2d, attn, hg, z, bf(w_br_attn[0]), bf(w_br_hgrn[0]), bf(w_out[0]), row(norm_x_g[0]),
               bf(wq_x[0]), kvm, bf(wo_x[0]), batch, seq, n_mem, tm)
    out = _ffn(h, row(norm_ffn_g[0]), bf(w_ffn_in[0]), bf(w_ffn_out[0]), row(final_norm_g), tm)
    return out.reshape(batch, seq, d)
```
